```python
import math
import jax, jax.numpy as jnp
from jax import lax
import numpy as np

D_MODEL = 4096
BATCH = 1
SEQ = 8192
DEPTH = 1

MIX_WIDTH = D_MODEL
DIFF_WIDTH = MIX_WIDTH // 2
RET_WIDTH = MIX_WIDTH - DIFF_WIDTH
DIFF_HEADS = 8
DIFF_HEAD_DIM = DIFF_WIDTH // (2 * DIFF_HEADS)
DIFF_V_DIM = 2 * DIFF_HEAD_DIM
RET_HEADS = 8
RET_HEAD_DIM = RET_WIDTH // RET_HEADS
IN_COLS = 3 * DIFF_WIDTH + 4 * RET_WIDTH
FFN_DIM = ((8 * D_MODEL // 3 + 255) // 256) * 256
CONV_WIDTH = 3
Q_BLOCK = 128
RET_CHUNK = 128
ROPE_BASE = 10000.0
LN_EPS = 1e-5
N_MOD = 6
DEEPNORM_ALPHA = (2.0 * DEPTH) ** 0.25
DEEPNORM_BETA = (8.0 * DEPTH) ** -0.25

kernel_name = 'hymba_style_diffattn_retention_convffn_deepnorm_adaln'


def lambda_init_fn(layer_idx):
    return 0.8 - 0.6 * math.exp(-0.3 * layer_idx)


def layer_norm(x, g, b):
    xf = x.astype(jnp.float32)
    mu = jnp.mean(xf, axis=-1, keepdims=True)
    var = jnp.mean(jnp.square(xf - mu), axis=-1, keepdims=True)
    y = (xf - mu) * lax.rsqrt(var + LN_EPS)
    return (y * g.astype(jnp.float32) + b.astype(jnp.float32)).astype(x.dtype)


def head_rms_norm(x, g):
    xf = x.astype(jnp.float32)
    y = xf * lax.rsqrt(jnp.mean(jnp.square(xf), axis=-1, keepdims=True) + LN_EPS)
    return (y * g.astype(jnp.float32)).astype(x.dtype)


def head_group_norm(x, g, b):
    B, S, H, dv = x.shape
    xf = x.astype(jnp.float32)
    mu = jnp.mean(xf, axis=-1, keepdims=True)
    var = jnp.mean(jnp.square(xf - mu), axis=-1, keepdims=True)
    y = ((xf - mu) * lax.rsqrt(var + LN_EPS)).reshape(B, S, H * dv)
    return (y * g.astype(jnp.float32) + b.astype(jnp.float32)).astype(x.dtype)


def rotary(x):
    S, d = x.shape[1], x.shape[-1]
    inv_freq = ROPE_BASE ** (-jnp.arange(0, d, 2, dtype=jnp.float32) / d)
    ang = jnp.arange(S, dtype=jnp.float32)[:, None] * inv_freq[None, :]
    cos = jnp.cos(ang)[None, :, None, :].astype(x.dtype)
    sin = jnp.sin(ang)[None, :, None, :].astype(x.dtype)
    x1, x2 = x[..., : d // 2], x[..., d // 2:]
    return jnp.concatenate([x1 * cos - x2 * sin, x2 * cos + x1 * sin], axis=-1)


def diff_attention(q, k, v, lam):
    B, S, H, _, d = q.shape
    n_blk = S // Q_BLOCK
    scale = d ** -0.5
    q_blocks = q.reshape(B, n_blk, Q_BLOCK, H, 2, d).transpose(1, 0, 3, 4, 2, 5)
    k_t = k.transpose(0, 2, 3, 1, 4)
    v_t = v.transpose(0, 2, 1, 3)
    key_pos = jnp.arange(S)
    neg = jnp.finfo(jnp.float32).min

    def block(args):
        q_blk, blk = args
        s = jnp.einsum('bhmqd,bhmkd->bhmqk', q_blk, k_t).astype(jnp.float32) * scale
        q_pos = blk * Q_BLOCK + jnp.arange(Q_BLOCK)
        causal = key_pos[None, :] <= q_pos[:, None]
        p = jax.nn.softmax(jnp.where(causal, s, neg), axis=-1)
        a = (p[:, :, 0] - lam * p[:, :, 1]).astype(v_t.dtype)
        return jnp.einsum('bhqk,bhkv->bhqv', a, v_t)

    o = lax.map(block, (q_blocks, jnp.arange(n_blk)))
    return o.transpose(1, 0, 3, 2, 4).reshape(B, S, H, v.shape[-1])


def retention_chunkwise(q, k, v):
    B, S, H, d = q.shape
    dv = v.shape[-1]
    C = RET_CHUNK
    n_chk = S // C
    dt = q.dtype
    log_gamma = jnp.log1p(-jnp.exp2(-5.0 - jnp.arange(H, dtype=jnp.float32)))
    idx = jnp.arange(C, dtype=jnp.float32)
    rel = idx[:, None] - idx[None, :]
    decay = jnp.where(rel >= 0, jnp.exp(log_gamma[:, None, None] * jnp.maximum(rel, 0.0)), 0.0).astype(dt)
    xi = jnp.exp(log_gamma[:, None] * (idx + 1.0)).astype(dt)
    zeta = jnp.exp(log_gamma[:, None] * (C - 1.0 - idx)).astype(dt)
    gamma_c = jnp.exp(log_gamma * C).astype(dt)

    def to_chunks(t):
        return t.reshape(B, n_chk, C, H, t.shape[-1]).transpose(1, 0, 3, 2, 4)

    def step(state, inp):
        qc, kc, vc = inp
        inner = jnp.einsum('bhid,bhjd->bhij', qc, kc) * decay
        o = (jnp.einsum('bhij,bhjv->bhiv', inner, vc)
             + jnp.einsum('bhid,bhdv->bhiv', qc, state) * xi[:, :, None])
        state = state * gamma_c[:, None, None] + jnp.einsum('bhjd,bhjv->bhdv', kc * zeta[:, :, None], vc)
        return state, o

    state0 = jnp.zeros((B, H, d, dv), dt)
    _, o = lax.scan(step, state0, (to_chunks(q), to_chunks(k), to_chunks(v)))
    return o.transpose(1, 0, 3, 2, 4).reshape(B, S, H, dv)


def causal_depthwise_conv(u, w, b):
    ch = u.shape[-1]
    y = lax.conv_general_dilated(
        u, w.reshape(CONV_WIDTH, 1, ch).astype(u.dtype),
        window_strides=(1,), padding=[(CONV_WIDTH - 1, 0)],
        dimension_numbers=('NWC', 'WIO', 'NWC'), feature_group_count=ch)
    return y + b.astype(u.dtype)


def setup_inputs(seed: int = 0) -> dict:
    key = jax.random.key(seed)
    ks = jax.random.split(key, 24)
    f32 = jnp.float32
    L, D = DEPTH, D_MODEL

    def nrm(k, shape, std):
        return jax.random.normal(k, shape, f32) * std

    col_scale = jnp.concatenate([
        jnp.ones((2 * DIFF_WIDTH,), f32), jnp.full((DIFF_WIDTH,), DEEPNORM_BETA, f32),
        jnp.ones((2 * RET_WIDTH,), f32), jnp.full((RET_WIDTH,), DEEPNORM_BETA, f32),
        jnp.ones((RET_WIDTH,), f32)])
    return {
        'x': nrm(ks[0], (BATCH, SEQ, D), 1.0),
        'c': nrm(ks[1], (BATCH, D), 1.0),
        'w_ada': nrm(ks[2], (L, D, N_MOD * D), D ** -0.5),
        'b_ada': nrm(ks[3], (L, N_MOD * D), 0.02),
        'w_in': nrm(ks[4], (L, D, IN_COLS), D ** -0.5) * col_scale,
        'lambda_q1': nrm(ks[5], (L, DIFF_HEAD_DIM), 0.1),
        'lambda_k1': nrm(ks[6], (L, DIFF_HEAD_DIM), 0.1),
        'lambda_q2': nrm(ks[7], (L, DIFF_HEAD_DIM), 0.1),
        'lambda_k2': nrm(ks[8], (L, DIFF_HEAD_DIM), 0.1),
        'diff_norm_g': 1.0 + nrm(ks[9], (L, DIFF_V_DIM), 0.02),
        'ret_norm_g': 1.0 + nrm(ks[10], (L, RET_WIDTH), 0.02),
        'ret_norm_b': nrm(ks[11], (L, RET_WIDTH), 0.02),
        'w_out': nrm(ks[12], (L, MIX_WIDTH, D), MIX_WIDTH ** -0.5 * DEEPNORM_BETA),
        'ln1_g': 1.0 + nrm(ks[13], (L, D), 0.02),
        'ln1_b': nrm(ks[14], (L, D), 0.02),
        'w_up': nrm(ks[15], (L, D, 2 * FFN_DIM), D ** -0.5 * DEEPNORM_BETA),
        'conv_w': nrm(ks[16], (L, CONV_WIDTH, 2 * FFN_DIM), CONV_WIDTH ** -0.5),
        'conv_b': nrm(ks[17], (L, 2 * FFN_DIM), 0.02),
        'w_down': nrm(ks[18], (L, FFN_DIM, D), FFN_DIM ** -0.5 * DEEPNORM_BETA),
        'ln2_g': 1.0 + nrm(ks[19], (L, D), 0.02),
        'ln2_b': nrm(ks[20], (L, D), 0.02),
    }


def reference(x, c, w_ada, b_ada, w_in, lambda_q1, lambda_k1, lambda_q2, lambda_k2,
              diff_norm_g, ret_norm_g, ret_norm_b, w_out, ln1_g, ln1_b,
              w_up, conv_w, conv_b, w_down, ln2_g, ln2_b):
    B, S, D = x.shape
    c_act = jax.nn.silu(c)
    splits = np.cumsum([DIFF_WIDTH, DIFF_WIDTH, DIFF_WIDTH, RET_WIDTH, RET_WIDTH, RET_WIDTH]).tolist()
    for l in range(DEPTH):
        lam_init = lambda_init_fn(l)
        mod = (c_act @ w_ada[l] + b_ada[l]).reshape(B, N_MOD, D)
        shift1, scale1, gate1 = mod[:, 0, None], mod[:, 1, None], mod[:, 2, None]
        shift2, scale2, gate2 = mod[:, 3, None], mod[:, 4, None], mod[:, 5, None]

        h = x * (1.0 + scale1) + shift1
        proj = h @ w_in[l]
        dq, dk, dv, rq, rk, rv, rg = jnp.split(proj, splits, axis=-1)

        lam = (jnp.exp(jnp.sum(lambda_q1[l].astype(jnp.float32) * lambda_k1[l].astype(jnp.float32)))
               - jnp.exp(jnp.sum(lambda_q2[l].astype(jnp.float32) * lambda_k2[l].astype(jnp.float32)))
               + lam_init)
        a_out = diff_attention(dq.reshape(B, S, DIFF_HEADS, 2, DIFF_HEAD_DIM),
                               dk.reshape(B, S, DIFF_HEADS, 2, DIFF_HEAD_DIM),
                               dv.reshape(B, S, DIFF_HEADS, DIFF_V_DIM), lam)
        a_out = (head_rms_norm(a_out, diff_norm_g[l]) * (1.0 - lam_init)).reshape(B, S, DIFF_WIDTH)

        q_r = rotary(rq.reshape(B, S, RET_HEADS, RET_HEAD_DIM))
        k_r = rotary(rk.reshape(B, S, RET_HEADS, RET_HEAD_DIM)) * (RET_HEAD_DIM ** -0.5)
        r_out = retention_chunkwise(q_r, k_r, rv.reshape(B, S, RET_HEADS, RET_HEAD_DIM))
        r_out = jax.nn.silu(rg) * head_group_norm(r_out, ret_norm_g[l], ret_norm_b[l])

        mix = jnp.concatenate([a_out, r_out], axis=-1) @ w_out[l]
        x = layer_norm(DEEPNORM_ALPHA * x + gate1 * mix, ln1_g[l], ln1_b[l])

        h = x * (1.0 + scale2) + shift2
        u = causal_depthwise_conv(h @ w_up[l], conv_w[l], conv_b[l])
        u_gate, u_val = jnp.split(u, 2, axis=-1)
        f = (jax.nn.silu(u_gate) * u_val) @ w_down[l]
        x = layer_norm(DEEPNORM_ALPHA * x + gate2 * f, ln2_g[l], ln2_b[l])
    return x
```

```python
import functools
import math

import jax
import jax.numpy as jnp
from jax import lax
from jax.experimental import pallas as pl
from jax.experimental.pallas import tpu as pltpu

F32 = jnp.float32
BF16 = jnp.bfloat16

D_MODEL = 4096
SEQ = 8192
DIFF_WIDTH = D_MODEL // 2
RET_WIDTH = D_MODEL - DIFF_WIDTH
DIFF_HEADS = 8
DIFF_HEAD_DIM = DIFF_WIDTH // (2 * DIFF_HEADS)
DIFF_V_DIM = 2 * DIFF_HEAD_DIM
RET_HEADS = 8
RET_HEAD_DIM = RET_WIDTH // RET_HEADS
FFN_DIM = ((8 * D_MODEL // 3 + 255) // 256) * 256
CONV_WIDTH = 3
RET_CHUNK = 128
ROPE_BASE = 10000.0
LN_EPS = 1e-5
N_MOD = 6
DEPTH = 1
DEEPNORM_ALPHA = (2.0 * DEPTH) ** 0.25
LAM_INIT = 0.8 - 0.6 * math.exp(-0.3 * 0)

V7X_VMEM_BYTES = 64 * 1024 * 1024
VMEM_LIMIT = V7X_VMEM_BYTES - 8 * 1024 * 1024

HEAD_COLS = 256
FFN_TILE = 1024
FFN_PAD = ((FFN_DIM + FFN_TILE - 1) // FFN_TILE) * FFN_TILE
MASK_VALUE = -1e30


def _params(*sem):
    return pltpu.CompilerParams(dimension_semantics=sem, vmem_limit_bytes=VMEM_LIMIT)


def _silu(v):
    return v / (1.0 + jnp.exp(-v))


def _ada_kernel(c_ref, w_ref, b_ref, o_ref):
    c = c_ref[...]
    o_ref[...] = jnp.sum(w_ref[...] * _silu(c), axis=0, keepdims=True) + b_ref[...]


def _ada(c_col, w_ada, b_ada):
    d, n = w_ada.shape
    tn = 512
    return pl.pallas_call(
        _ada_kernel,
        grid=(n // tn,),
        in_specs=[pl.BlockSpec((d, 1), lambda j: (0, 0)),
                  pl.BlockSpec((d, tn), lambda j: (0, j)),
                  pl.BlockSpec((1, tn), lambda j: (0, j))],
        out_specs=pl.BlockSpec((1, tn), lambda j: (0, j)),
        out_shape=jax.ShapeDtypeStruct((1, n), F32),
        compiler_params=_params("arbitrary"),
        name="ada_mod",
    )(c_col, w_ada, b_ada)


def _modulate_kernel(x_ref, mod_ref, o_ref, *, shift_row, scale_row):
    shift = mod_ref[shift_row:shift_row + 1, :]
    scale = mod_ref[scale_row:scale_row + 1, :]
    o_ref[...] = (x_ref[...] * (1.0 + scale) + shift).astype(BF16)


def _modulate(x, mod, shift_row, scale_row):
    s, d = x.shape
    tm = 512
    return pl.pallas_call(
        functools.partial(_modulate_kernel, shift_row=shift_row, scale_row=scale_row),
        grid=(s // tm,),
        in_specs=[pl.BlockSpec((tm, d), lambda i: (i, 0)),
                  pl.BlockSpec((N_MOD, d), lambda i: (0, 0))],
        out_specs=pl.BlockSpec((tm, d), lambda i: (i, 0)),
        out_shape=jax.ShapeDtypeStruct((s, d), BF16),
        compiler_params=_params("arbitrary"),
        name="modulate1",
    )(x, mod)


def _inproj_kernel(a_ref, w_ref, cos_ref, sin_ref, o_ref, *, tn):
    j = pl.program_id(0)
    per_group = DIFF_WIDTH // tn
    acc = jnp.dot(a_ref[...], w_ref[...], preferred_element_type=F32)
    is_rot = jnp.logical_and(j >= 3 * per_group, j < 5 * per_group)

    @pl.when(jnp.logical_not(is_rot))
    def _():
        qscale = jnp.where(j < per_group, DIFF_HEAD_DIM ** -0.5, 1.0).astype(F32)
        o_ref[...] = (acc * qscale).astype(BF16)

    @pl.when(is_rot)
    def _():
        kscale = jnp.where(j >= 4 * per_group, RET_HEAD_DIM ** -0.5, 1.0).astype(F32)
        cos = cos_ref[...] * kscale
        sin = sin_ref[...] * kscale
        half = RET_HEAD_DIM // 2
        for h in range(tn // RET_HEAD_DIM):
            lo = h * RET_HEAD_DIM
            x1 = acc[:, lo:lo + half]
            x2 = acc[:, lo + half:lo + 2 * half]
            o_ref[:, lo:lo + half] = (x1 * cos - x2 * sin).astype(BF16)
            o_ref[:, lo + half:lo + 2 * half] = (x2 * cos + x1 * sin).astype(BF16)


def _inproj(h, w_bf16, cos, sin):
    s, k = h.shape
    n = 3 * DIFF_WIDTH + 3 * RET_WIDTH
    tm, tn = 512, 1024
    return pl.pallas_call(
        functools.partial(_inproj_kernel, tn=tn),
        grid=(n // tn, s // tm),
        in_specs=[pl.BlockSpec((tm, k), lambda j, i: (i, 0)),
                  pl.BlockSpec((k, tn), lambda j, i: (0, j)),
                  pl.BlockSpec((tm, RET_HEAD_DIM // 2), lambda j, i: (i, 0)),
                  pl.BlockSpec((tm, RET_HEAD_DIM // 2), lambda j, i: (i, 0))],
        out_specs=pl.BlockSpec((tm, tn), lambda j, i: (i, j)),
        out_shape=jax.ShapeDtypeStruct((s, n), BF16),
        compiler_params=_params("arbitrary", "arbitrary"),
        name="in_proj",
    )(h, w_bf16, cos, sin)


def _gateproj_kernel(a_ref, w_ref, o_ref):
    o_ref[...] = jnp.dot(a_ref[...], w_ref[...], preferred_element_type=F32)


def _gateproj(h, w_bf16):
    s, k = h.shape
    tm, tn = 512, 1024
    col0 = (3 * DIFF_WIDTH + 3 * RET_WIDTH) // tn
    return pl.pallas_call(
        _gateproj_kernel,
        grid=(RET_WIDTH // tn, s // tm),
        in_specs=[pl.BlockSpec((tm, k), lambda j, i: (i, 0)),
                  pl.BlockSpec((k, tn), lambda j, i: (0, col0 + j))],
        out_specs=pl.BlockSpec((tm, tn), lambda j, i: (i, j)),
        out_shape=jax.ShapeDtypeStruct((s, RET_WIDTH), F32),
        compiler_params=_params("arbitrary", "arbitrary"),
        name="gate_proj",
    )(h, w_bf16)


def _attn_kernel(q_ref, k_ref, v_ref, g_ref, lq1_ref, lk1_ref, lq2_ref, lk2_ref, o_ref, *, blk):
    i = pl.program_id(1)
    d = DIFF_HEAD_DIM
    q = (q_ref[:, 0:d], q_ref[:, d:2 * d])

    def step(j, carry, masked):
        start = pl.multiple_of(j * blk, blk)
        kb = k_ref[pl.ds(start, blk), :]
        vb = v_ref[pl.ds(start, blk), :]
        new = []
        for m in range(2):
            m_old, l_old, acc_old = carry[m]
            s = lax.dot_general(q[m], kb[:, m * d:(m + 1) * d], (((1,), (1,)), ((), ())),
                                preferred_element_type=F32)
            if masked:
                row = lax.broadcasted_iota(jnp.int32, s.shape, 0)
                col = lax.broadcasted_iota(jnp.int32, s.shape, 1)
                s = jnp.where(col <= row, s, MASK_VALUE)
            m_new = jnp.maximum(m_old, jnp.max(s, axis=-1, keepdims=True))
            alpha = jnp.exp(m_old - m_new)
            p = jnp.exp(s - m_new)
            l_new = alpha * l_old + jnp.sum(p, axis=-1, keepdims=True)
            acc_new = alpha * acc_old + jnp.dot(p.astype(BF16), vb, preferred_element_type=F32)
            new.append((m_new, l_new, acc_new))
        return tuple(new)

    init_one = (jnp.full((blk, 1), MASK_VALUE, F32), jnp.zeros((blk, 1), F32),
                jnp.zeros((blk, DIFF_V_DIM), F32))
    carry = lax.fori_loop(0, i, lambda j, c: step(j, c, False), (init_one, init_one))
    (_, l0, acc0), (_, l1, acc1) = step(i, carry, True)

    lam = (jnp.exp(jnp.sum(lq1_ref[...] * lk1_ref[...], axis=-1, keepdims=True))
           - jnp.exp(jnp.sum(lq2_ref[...] * lk2_ref[...], axis=-1, keepdims=True)) + LAM_INIT)
    out = acc0 / l0 - lam * (acc1 / l1)
    y = out * lax.rsqrt(jnp.mean(out * out, axis=-1, keepdims=True) + LN_EPS)
    o_ref[...] = (y * g_ref[...] * (1.0 - LAM_INIT)).astype(BF16)


def _diff_attention(qkv, g, lq1, lk1, lq2, lk2):
    s = qkv.shape[0]
    blk = 512
    k_col0 = DIFF_WIDTH // HEAD_COLS
    v_col0 = 2 * DIFF_WIDTH // HEAD_COLS
    vec = pl.BlockSpec((1, DIFF_HEAD_DIM), lambda h, i: (0, 0))
    return pl.pallas_call(
        functools.partial(_attn_kernel, blk=blk),
        grid=(DIFF_HEADS, s // blk),
        in_specs=[pl.BlockSpec((blk, HEAD_COLS), lambda h, i: (i, h)),
                  pl.BlockSpec((s, HEAD_COLS), lambda h, i: (0, k_col0 + h)),
                  pl.BlockSpec((s, HEAD_COLS), lambda h, i: (0, v_col0 + h)),
                  pl.BlockSpec((1, DIFF_V_DIM), lambda h, i: (0, 0)),
                  vec, vec, vec, vec],
        out_specs=pl.BlockSpec((blk, DIFF_V_DIM), lambda h, i: (i, h)),
        out_shape=jax.ShapeDtypeStruct((s, DIFF_WIDTH), BF16),
        compiler_params=_params("arbitrary", "arbitrary"),
        name="diff_attn",
    )(qkv, qkv, qkv, g, lq1, lk1, lq2, lk2)


def _retention_kernel(q_ref, k_ref, v_ref, rg_ref, decay_ref, xi_ref, zeta_ref, gam_ref,
                      g_ref, b_ref, o_ref, state_ref):
    @pl.when(pl.program_id(0) == 0)
    def _():
        state_ref[...] = jnp.zeros_like(state_ref)

    dh = RET_HEAD_DIM
    for h in range(RET_HEADS):
        cols = slice(h * dh, (h + 1) * dh)
        qc = q_ref[:, cols]
        kc = k_ref[:, cols]
        vc = v_ref[:, cols]
        st = state_ref[h]
        inner = lax.dot_general(qc, kc, (((1,), (1,)), ((), ())),
                                preferred_element_type=F32) * decay_ref[h]
        o = (jnp.dot(inner.astype(BF16), vc, preferred_element_type=F32)
             + jnp.dot(qc, st.astype(BF16), preferred_element_type=F32) * xi_ref[h])
        kz = (kc.astype(F32) * zeta_ref[h]).astype(BF16)
        state_ref[h] = st * gam_ref[h] + lax.dot_general(
            kz, vc, (((0,), (0,)), ((), ())), preferred_element_type=F32)
        mu = jnp.mean(o, axis=-1, keepdims=True)
        cen = o - mu
        var = jnp.mean(cen * cen, axis=-1, keepdims=True)
        y = cen * lax.rsqrt(var + LN_EPS) * g_ref[:, cols] + b_ref[:, cols]
        o_ref[:, cols] = (_silu(rg_ref[:, cols]) * y).astype(BF16)


def _retention(qkv, rg, g, b):
    s = qkv.shape[0]
    c = RET_CHUNK
    hh = RET_HEADS
    log_gamma = jnp.log1p(-jnp.exp2(-5.0 - jnp.arange(hh, dtype=F32)))
    idx = jnp.arange(c, dtype=F32)
    rel = idx[:, None] - idx[None, :]
    decay = jnp.where(rel >= 0, jnp.exp(log_gamma[:, None, None] * jnp.maximum(rel, 0.0)), 0.0)
    xi = jnp.exp(log_gamma[:, None] * (idx + 1.0))[:, :, None]
    zeta = jnp.exp(log_gamma[:, None] * (c - 1.0 - idx))[:, :, None]
    gam = jnp.broadcast_to(jnp.exp(log_gamma * c)[:, None, None], (hh, 1, RET_HEAD_DIM))
    q_blk = 3 * DIFF_WIDTH // RET_WIDTH
    full3 = lambda shape: pl.BlockSpec(shape, lambda n: (0, 0, 0))
    return pl.pallas_call(
        _retention_kernel,
        grid=(s // c,),
        in_specs=[pl.BlockSpec((c, RET_WIDTH), lambda n: (n, q_blk)),
                  pl.BlockSpec((c, RET_WIDTH), lambda n: (n, q_blk + 1)),
                  pl.BlockSpec((c, RET_WIDTH), lambda n: (n, q_blk + 2)),
                  pl.BlockSpec((c, RET_WIDTH), lambda n: (n, 0)),
                  full3((hh, c, c)), full3((hh, c, 1)), full3((hh, c, 1)),
                  full3((hh, 1, RET_HEAD_DIM)),
                  pl.BlockSpec((1, RET_WIDTH), lambda n: (0, 0)),
                  pl.BlockSpec((1, RET_WIDTH), lambda n: (0, 0))],
        out_specs=pl.BlockSpec((c, RET_WIDTH), lambda n: (n, 0)),
        out_shape=jax.ShapeDtypeStruct((s, RET_WIDTH), BF16),
        scratch_shapes=[pltpu.VMEM((hh, RET_HEAD_DIM, RET_HEAD_DIM), F32)],
        compiler_params=_params("arbitrary"),
        name="retention",
    )(qkv, qkv, qkv, rg, decay, xi, zeta, gam, g, b)


def _outproj_kernel(a1_ref, a2_ref, w1_ref, w2_ref, x_ref, mod_ref, o_ref, *, gate_row):
    mix = (jnp.dot(a1_ref[...], w1_ref[...], preferred_element_type=F32)
           + jnp.dot(a2_ref[...], w2_ref[...], preferred_element_type=F32))
    gate = mod_ref[gate_row:gate_row + 1, :]
    o_ref[...] = DEEPNORM_ALPHA * x_ref[...] + gate * mix


def _outproj(a1, a2, w_bf16, x, mod):
    s, d = x.shape
    tm, tn = 512, 1024
    half = a1.shape[1]
    return pl.pallas_call(
        functools.partial(_outproj_kernel, gate_row=2),
        grid=(d // tn, s // tm),
        in_specs=[pl.BlockSpec((tm, half), lambda j, i: (i, 0)),
                  pl.BlockSpec((tm, half), lambda j, i: (i, 0)),
                  pl.BlockSpec((half, tn), lambda j, i: (0, j)),
                  pl.BlockSpec((half, tn), lambda j, i: (1, j)),
                  pl.BlockSpec((tm, tn), lambda j, i: (i, j)),
                  pl.BlockSpec((N_MOD, tn), lambda j, i: (0, j))],
        out_specs=pl.BlockSpec((tm, tn), lambda j, i: (i, j)),
        out_shape=jax.ShapeDtypeStruct((s, d), F32),
        compiler_params=_params("arbitrary", "arbitrary"),
        name="out_proj",
    )(a1, a2, w_bf16, w_bf16, x, mod)


def _layer_norm(y, g, b):
    mu = jnp.mean(y, axis=-1, keepdims=True)
    cen = y - mu
    var = jnp.mean(cen * cen, axis=-1, keepdims=True)
    return cen * lax.rsqrt(var + LN_EPS) * g + b


def _ln_mod_kernel(y_ref, g_ref, b_ref, mod_ref, x_ref, h_ref, *, shift_row, scale_row):
    xn = _layer_norm(y_ref[...], g_ref[...], b_ref[...])
    x_ref[...] = xn
    shift = mod_ref[shift_row:shift_row + 1, :]
    scale = mod_ref[scale_row:scale_row + 1, :]
    h_ref[...] = (xn * (1.0 + scale) + shift).astype(BF16)


def _ln_mod(y, g, b, mod):
    s, d = y.shape
    tm = 256
    row = pl.BlockSpec((tm, d), lambda i: (i, 0))
    vec = pl.BlockSpec((1, d), lambda i: (0, 0))
    return pl.pallas_call(
        functools.partial(_ln_mod_kernel, shift_row=3, scale_row=4),
        grid=(s // tm,),
        in_specs=[row, vec, vec, pl.BlockSpec((N_MOD, d), lambda i: (0, 0))],
        out_specs=[row, row],
        out_shape=[jax.ShapeDtypeStruct((s, d), F32), jax.ShapeDtypeStruct((s, d), BF16)],
        compiler_params=_params("arbitrary"),
        name="ln1_modulate2",
    )(y, g, b, mod)


def _ln_kernel(y_ref, g_ref, b_ref, o_ref):
    o_ref[...] = _layer_norm(y_ref[...], g_ref[...], b_ref[...])


def _ln(y, g, b):
    s, d = y.shape
    tm = 256
    row = pl.BlockSpec((tm, d), lambda i: (i, 0))
    vec = pl.BlockSpec((1, d), lambda i: (0, 0))
    return pl.pallas_call(
        _ln_kernel,
        grid=(s // tm,),
        in_specs=[row, vec, vec],
        out_specs=row,
        out_shape=jax.ShapeDtypeStruct((s, d), F32),
        compiler_params=_params("arbitrary"),
        name="ln2",
    )(y, g, b)


def _upproj_kernel(a_ref, wg_ref, wv_ref, cwg_ref, cwv_ref, cbg_ref, cbv_ref, o_ref,
                   ug_ref, uv_ref, *, tm):
    first = pl.program_id(1) == 0

    def conv(u_ref, w_ref, cw_ref, cb_ref):
        @pl.when(first)
        def _():
            u_ref[0:8, :] = jnp.zeros((8, u_ref.shape[1]), F32)

        @pl.when(jnp.logical_not(first))
        def _():
            u_ref[0:8, :] = u_ref[tm:tm + 8, :]

        u_ref[8:tm + 8, :] = jnp.dot(a_ref[...], w_ref[...], preferred_element_type=F32)
        return (cw_ref[0:1, :] * u_ref[6:tm + 6, :] + cw_ref[1:2, :] * u_ref[7:tm + 7, :]
                + cw_ref[2:3, :] * u_ref[8:tm + 8, :] + cb_ref[...])

    gate = conv(ug_ref, wg_ref, cwg_ref, cbg_ref)
    val = conv(uv_ref, wv_ref, cwv_ref, cbv_ref)
    o_ref[...] = (_silu(gate) * val).astype(BF16)


def _upproj(h, w_bf16, conv_w, conv_b):
    s, k = h.shape
    tm, tn = 1024, 512
    nj = FFN_PAD // tn
    return pl.pallas_call(
        functools.partial(_upproj_kernel, tm=tm),
        grid=(nj, s // tm),
        in_specs=[pl.BlockSpec((tm, k), lambda j, i: (i, 0)),
                  pl.BlockSpec((k, tn), lambda j, i: (0, j)),
                  pl.BlockSpec((k, tn), lambda j, i: (0, nj + j)),
                  pl.BlockSpec((CONV_WIDTH, tn), lambda j, i: (0, j)),
                  pl.BlockSpec((CONV_WIDTH, tn), lambda j, i: (0, nj + j)),
                  pl.BlockSpec((1, tn), lambda j, i: (0, j)),
                  pl.BlockSpec((1, tn), lambda j, i: (0, nj + j))],
        out_specs=pl.BlockSpec((tm, tn), lambda j, i: (i, j)),
        out_shape=jax.ShapeDtypeStruct((s, FFN_PAD), BF16),
        scratch_shapes=[pltpu.VMEM((tm + 8, tn), F32), pltpu.VMEM((tm + 8, tn), F32)],
        compiler_params=_params("arbitrary", "arbitrary"),
        name="up_proj_conv_gate",
    )(h, w_bf16, w_bf16, conv_w, conv_w, conv_b, conv_b)


def _downproj_kernel(a_ref, w_ref, x_ref, mod_ref, o_ref, *, gate_row):
    f = jnp.dot(a_ref[...], w_ref[...], preferred_element_type=F32)
    gate = mod_ref[gate_row:gate_row + 1, :]
    o_ref[...] = DEEPNORM_ALPHA * x_ref[...] + gate * f


def _downproj(a, w_bf16, x, mod):
    s, d = x.shape
    k = a.shape[1]
    tm, tn = 512, 512
    return pl.pallas_call(
        functools.partial(_downproj_kernel, gate_row=5),
        grid=(d // tn, s // tm),
        in_specs=[pl.BlockSpec((tm, k), lambda j, i: (i, 0)),
                  pl.BlockSpec((k, tn), lambda j, i: (0, j)),
                  pl.BlockSpec((tm, tn), lambda j, i: (i, j)),
                  pl.BlockSpec((N_MOD, tn), lambda j, i: (0, j))],
        out_specs=pl.BlockSpec((tm, tn), lambda j, i: (i, j)),
        out_shape=jax.ShapeDtypeStruct((s, d), F32),
        compiler_params=_params("arbitrary", "arbitrary"),
        name="down_proj",
    )(a, w_bf16, x, mod)


def _rotary_tables(seq, d):
    inv_freq = ROPE_BASE ** (-jnp.arange(0, d, 2, dtype=F32) / d)
    ang = jnp.arange(seq, dtype=F32)[:, None] * inv_freq[None, :]
    return jnp.cos(ang), jnp.sin(ang)


def _pad_cols(a, width):
    return jnp.pad(a, ((0, 0), (0, width - a.shape[1])))


def _pad_ffn_halves(a):
    return jnp.concatenate([_pad_cols(a[:, :FFN_DIM], FFN_PAD), _pad_cols(a[:, FFN_DIM:], FFN_PAD)],
                           axis=1)


def kernel(x, c, w_ada, b_ada, w_in, lambda_q1, lambda_k1, lambda_q2, lambda_k2, diff_norm_g,
           ret_norm_g, ret_norm_b, w_out, ln1_g, ln1_b, w_up, conv_w, conv_b, w_down, ln2_g, ln2_b):
    batch, seq, d = x.shape
    assert (batch, seq, d) == (1, SEQ, D_MODEL) and w_ada.shape[0] == DEPTH
    l = 0
    x2 = x.reshape(seq, d)
    cos, sin = _rotary_tables(seq, RET_HEAD_DIM)

    w_in_b = w_in[l].astype(BF16)
    w_out_b = w_out[l].astype(BF16)
    w_up_b = _pad_ffn_halves(w_up[l]).astype(BF16)
    w_down_b = jnp.pad(w_down[l], ((0, FFN_PAD - FFN_DIM), (0, 0))).astype(BF16)
    conv_w_p = _pad_ffn_halves(conv_w[l])
    conv_b_p = _pad_ffn_halves(conv_b[l][None, :])

    mod = _ada(c.reshape(d, 1), w_ada[l], b_ada[l][None, :]).reshape(N_MOD, d)

    h1 = _modulate(x2, mod, shift_row=0, scale_row=1)
    qkv = _inproj(h1, w_in_b, cos, sin)
    rg = _gateproj(h1, w_in_b)
    a_out = _diff_attention(qkv, diff_norm_g[l][None, :], lambda_q1[l][None, :], lambda_k1[l][None, :],
                            lambda_q2[l][None, :], lambda_k2[l][None, :])
    r_out = _retention(qkv, rg, ret_norm_g[l][None, :], ret_norm_b[l][None, :])
    y1 = _outproj(a_out, r_out, w_out_b, x2, mod)
    x1, h2 = _ln_mod(y1, ln1_g[l][None, :], ln1_b[l][None, :], mod)

    gated = _upproj(h2, w_up_b, conv_w_p, conv_b_p)
    y2 = _downproj(gated, w_down_b, x1, mod)
    out = _ln(y2, ln2_g[l][None, :], ln2_b[l][None, :])
    return out.reshape(batch, seq, d)
```

```python
import functools
import math

import jax
import jax.numpy as jnp
from jax import lax
from jax.experimental import pallas as pl
from jax.experimental.pallas import tpu as pltpu

F32 = jnp.float32
BF16 = jnp.bfloat16

D_MODEL = 4096
SEQ = 8192
DIFF_WIDTH = D_MODEL // 2
RET_WIDTH = D_MODEL - DIFF_WIDTH
DIFF_HEADS = 8
DIFF_HEAD_DIM = DIFF_WIDTH // (2 * DIFF_HEADS)
DIFF_V_DIM = 2 * DIFF_HEAD_DIM
RET_HEADS = 8
RET_HEAD_DIM = RET_WIDTH // RET_HEADS
FFN_DIM = ((8 * D_MODEL // 3 + 255) // 256) * 256
CONV_WIDTH = 3
RET_CHUNK = 128
ROPE_BASE = 10000.0
LN_EPS = 1e-5
N_MOD = 6
DEPTH = 1
DEEPNORM_ALPHA = (2.0 * DEPTH) ** 0.25
LAM_INIT = 0.8 - 0.6 * math.exp(-0.3 * 0)

V7X_VMEM_BYTES = 64 * 1024 * 1024
VMEM_LIMIT = V7X_VMEM_BYTES - 8 * 1024 * 1024

HEAD_COLS = 256
FFN_TILE = 256
MASK_VALUE = -1e30


def _params(*sem):
    return pltpu.CompilerParams(dimension_semantics=sem, vmem_limit_bytes=VMEM_LIMIT)


def _silu(v):
    return v / (1.0 + jnp.exp(-v))


def _ada_kernel(c_ref, w_ref, b_ref, o_ref):
    c = c_ref[...]
    o_ref[...] = jnp.sum(w_ref[...] * _silu(c), axis=0, keepdims=True) + b_ref[...]


def _ada(c_col, w_ada, b_ada):
    d, n = w_ada.shape
    tn = 512
    return pl.pallas_call(
        _ada_kernel,
        grid=(n // tn,),
        in_specs=[pl.BlockSpec((d, 1), lambda j: (0, 0)),
                  pl.BlockSpec((d, tn), lambda j: (0, j)),
                  pl.BlockSpec((1, tn), lambda j: (0, j))],
        out_specs=pl.BlockSpec((1, tn), lambda j: (0, j)),
        out_shape=jax.ShapeDtypeStruct((1, n), F32),
        compiler_params=_params("arbitrary"),
        name="ada_mod",
    )(c_col, w_ada, b_ada)


def _modulate_kernel(x_ref, mod_ref, o_ref, *, shift_row, scale_row):
    shift = mod_ref[shift_row:shift_row + 1, :]
    scale = mod_ref[scale_row:scale_row + 1, :]
    o_ref[...] = (x_ref[...] * (1.0 + scale) + shift).astype(BF16)


def _modulate(x, mod, shift_row, scale_row):
    s, d = x.shape
    tm = 512
    return pl.pallas_call(
        functools.partial(_modulate_kernel, shift_row=shift_row, scale_row=scale_row),
        grid=(s // tm,),
        in_specs=[pl.BlockSpec((tm, d), lambda i: (i, 0)),
                  pl.BlockSpec((N_MOD, d), lambda i: (0, 0))],
        out_specs=pl.BlockSpec((tm, d), lambda i: (i, 0)),
        out_shape=jax.ShapeDtypeStruct((s, d), BF16),
        compiler_params=_params("arbitrary"),
        name="modulate1",
    )(x, mod)


def _cast_weight_tile(w_ref, wb_ref):
    @pl.when(pl.program_id(1) == 0)
    def _():
        wb_ref[...] = w_ref[...].astype(BF16)


def _inproj_kernel(a_ref, w_ref, cos_ref, sin_ref, o_ref, wb_ref, *, tn):
    j = pl.program_id(0)
    per_group = DIFF_WIDTH // tn
    _cast_weight_tile(w_ref, wb_ref)
    acc = jnp.dot(a_ref[...], wb_ref[...], preferred_element_type=F32)
    is_rot = jnp.logical_and(j >= 3 * per_group, j < 5 * per_group)

    @pl.when(jnp.logical_not(is_rot))
    def _():
        qscale = jnp.where(j < per_group, DIFF_HEAD_DIM ** -0.5 * math.log2(math.e), 1.0).astype(F32)
        o_ref[...] = (acc * qscale).astype(BF16)

    @pl.when(is_rot)
    def _():
        kscale = jnp.where(j >= 4 * per_group, RET_HEAD_DIM ** -0.5, 1.0).astype(F32)
        cos = cos_ref[...] * kscale
        sin = sin_ref[...] * kscale
        half = RET_HEAD_DIM // 2
        for h in range(tn // RET_HEAD_DIM):
            lo = h * RET_HEAD_DIM
            x1 = acc[:, lo:lo + half]
            x2 = acc[:, lo + half:lo + 2 * half]
            o_ref[:, lo:lo + half] = (x1 * cos - x2 * sin).astype(BF16)
            o_ref[:, lo + half:lo + 2 * half] = (x2 * cos + x1 * sin).astype(BF16)


def _inproj(h, w, cos, sin):
    s, k = h.shape
    n = 3 * DIFF_WIDTH + 3 * RET_WIDTH
    tm, tn = 1024, 512
    return pl.pallas_call(
        functools.partial(_inproj_kernel, tn=tn),
        grid=(n // tn, s // tm),
        in_specs=[pl.BlockSpec((tm, k), lambda j, i: (i, 0)),
                  pl.BlockSpec((k, tn), lambda j, i: (0, j)),
                  pl.BlockSpec((tm, RET_HEAD_DIM // 2), lambda j, i: (i, 0)),
                  pl.BlockSpec((tm, RET_HEAD_DIM // 2), lambda j, i: (i, 0))],
        out_specs=pl.BlockSpec((tm, tn), lambda j, i: (i, j)),
        out_shape=jax.ShapeDtypeStruct((s, n), BF16),
        scratch_shapes=[pltpu.VMEM((k, tn), BF16)],
        compiler_params=_params("arbitrary", "arbitrary"),
        name="in_proj",
    )(h, w, cos, sin)


def _gateproj_kernel(a_ref, w_ref, o_ref, wb_ref):
    _cast_weight_tile(w_ref, wb_ref)
    o_ref[...] = jnp.dot(a_ref[...], wb_ref[...], preferred_element_type=F32)


def _gateproj(h, w):
    s, k = h.shape
    tm, tn = 1024, 512
    col0 = (3 * DIFF_WIDTH + 3 * RET_WIDTH) // tn
    return pl.pallas_call(
        _gateproj_kernel,
        grid=(RET_WIDTH // tn, s // tm),
        in_specs=[pl.BlockSpec((tm, k), lambda j, i: (i, 0)),
                  pl.BlockSpec((k, tn), lambda j, i: (0, col0 + j))],
        out_specs=pl.BlockSpec((tm, tn), lambda j, i: (i, j)),
        out_shape=jax.ShapeDtypeStruct((s, RET_WIDTH), F32),
        scratch_shapes=[pltpu.VMEM((k, tn), BF16)],
        compiler_params=_params("arbitrary", "arbitrary"),
        name="gate_proj",
    )(h, w)


def _attn_kernel(q_ref, k_ref, v_ref, g_ref, lq1_ref, lk1_ref, lq2_ref, lk2_ref, o_ref, vt_ref,
                 *, tq, tk):
    i = pl.program_id(1)
    d = DIFF_HEAD_DIM
    nt = (((1,), (1,)), ((), ()))
    q = (q_ref[:, 0:d], q_ref[:, d:2 * d])

    @pl.when(i == 0)
    def _():
        vt_ref[...] = v_ref[...].T

    def scores(j):
        start = pl.multiple_of(j * tk, tk)
        kb = k_ref[pl.ds(start, tk), :]
        sp = tuple(lax.dot_general(kb[:, m * d:(m + 1) * d], q[m], nt, preferred_element_type=F32)
                   for m in range(2))
        return sp, tuple(jnp.max(x, axis=0, keepdims=True) for x in sp)

    def consume(j, s_pair, carry, masked):
        start = pl.multiple_of(j * tk, tk)
        vtb = vt_ref[:, pl.ds(start, tk)]
        new = []
        for m in range(2):
            m_old, l_old, acc_old = carry[m]
            s = s_pair[0][m]
            if masked:
                key = lax.broadcasted_iota(jnp.int32, s.shape, 0)
                qry = lax.broadcasted_iota(jnp.int32, s.shape, 1)
                s = jnp.where(key - qry <= i * tq - j * tk, s, MASK_VALUE)
                m_new = jnp.maximum(m_old, jnp.max(s, axis=0, keepdims=True))
            else:
                m_new = jnp.maximum(m_old, s_pair[1][m])
            alpha = jnp.exp2(m_old - m_new)
            p = jnp.exp2(s - m_new)
            l_new = alpha * l_old + jnp.sum(p, axis=0, keepdims=True)
            acc_new = alpha * acc_old + jnp.dot(vtb, p.astype(BF16), preferred_element_type=F32)
            new.append((m_new, l_new, acc_new))
        return tuple(new)

    init_one = (jnp.full((1, tq), MASK_VALUE, F32), jnp.zeros((1, tq), F32),
                jnp.zeros((DIFF_V_DIM, tq), F32))
    n_full = (i * tq) // tk
    n_diag = max(1, tq // tk)

    def body(j, c):
        s_cur, carry = c
        s_next = scores(j + 1)
        return s_next, consume(j, s_cur, carry, False)

    s_cur, carry = lax.fori_loop(0, n_full, body, (scores(0), (init_one, init_one)))
    for dd in range(n_diag):
        s_next = scores(n_full + dd + 1) if dd + 1 < n_diag else None
        carry = consume(n_full + dd, s_cur, carry, True)
        s_cur = s_next
    (_, l0, acc0), (_, l1, acc1) = carry

    lam = (jnp.exp(jnp.sum(lq1_ref[...] * lk1_ref[...], axis=-1, keepdims=True))
           - jnp.exp(jnp.sum(lq2_ref[...] * lk2_ref[...], axis=-1, keepdims=True)) + LAM_INIT)
    out = acc0 / l0 - lam * (acc1 / l1)
    y = out * lax.rsqrt(jnp.mean(out * out, axis=0, keepdims=True) + LN_EPS)
    o_ref[...] = (y * g_ref[...] * (1.0 - LAM_INIT)).T.astype(BF16)


def _diff_attention(qkv, g_col, lq1, lk1, lq2, lk2):
    s = qkv.shape[0]
    tq, tk = 512, 512
    k_col0 = DIFF_WIDTH // HEAD_COLS
    v_col0 = 2 * DIFF_WIDTH // HEAD_COLS
    vec = pl.BlockSpec((1, DIFF_HEAD_DIM), lambda h, i: (0, 0))
    return pl.pallas_call(
        functools.partial(_attn_kernel, tq=tq, tk=tk),
        grid=(DIFF_HEADS, s // tq),
        in_specs=[pl.BlockSpec((tq, HEAD_COLS), lambda h, i: (i, h)),
                  pl.BlockSpec((s, HEAD_COLS), lambda h, i: (0, k_col0 + h)),
                  pl.BlockSpec((s, HEAD_COLS), lambda h, i: (0, v_col0 + h)),
                  pl.BlockSpec((DIFF_V_DIM, 1), lambda h, i: (0, 0)),
                  vec, vec, vec, vec],
        out_specs=pl.BlockSpec((tq, DIFF_V_DIM), lambda h, i: (i, h)),
        out_shape=jax.ShapeDtypeStruct((s, DIFF_WIDTH), BF16),
        scratch_shapes=[pltpu.VMEM((DIFF_V_DIM, s), BF16)],
        compiler_params=_params("arbitrary", "arbitrary"),
        name="diff_attn",
    )(qkv, qkv, qkv, g_col, lq1, lk1, lq2, lk2)


def _retention_kernel(q_ref, k_ref, v_ref, rg_ref, decay_ref, xi_ref, zeta_ref, gam_ref,
                      g_ref, b_ref, o_ref, state_ref):
    @pl.when(pl.program_id(0) == 0)
    def _():
        state_ref[...] = jnp.zeros_like(state_ref)

    dh = RET_HEAD_DIM
    for h in range(RET_HEADS):
        cols = slice(h * dh, (h + 1) * dh)
        qc = q_ref[:, cols]
        kc = k_ref[:, cols]
        vc = v_ref[:, cols]
        st = state_ref[h]
        inner = lax.dot_general(qc, kc, (((1,), (1,)), ((), ())),
                                preferred_element_type=F32) * decay_ref[h]
        o = (jnp.dot(inner.astype(BF16), vc, preferred_element_type=F32)
             + jnp.dot(qc, st.astype(BF16), preferred_element_type=F32) * xi_ref[h])
        kz = (kc.astype(F32) * zeta_ref[h]).astype(BF16)
        state_ref[h] = st * gam_ref[h] + lax.dot_general(
            kz, vc, (((0,), (0,)), ((), ())), preferred_element_type=F32)
        mu = jnp.mean(o, axis=-1, keepdims=True)
        cen = o - mu
        var = jnp.mean(cen * cen, axis=-1, keepdims=True)
        y = cen * lax.rsqrt(var + LN_EPS) * g_ref[:, cols] + b_ref[:, cols]
        o_ref[:, cols] = (_silu(rg_ref[:, cols]) * y).astype(BF16)


def _retention(qkv, rg, g, b):
    s = qkv.shape[0]
    c = RET_CHUNK
    hh = RET_HEADS
    log_gamma = jnp.log1p(-jnp.exp2(-5.0 - jnp.arange(hh, dtype=F32)))
    idx = jnp.arange(c, dtype=F32)
    rel = idx[:, None] - idx[None, :]
    decay = jnp.where(rel >= 0, jnp.exp(log_gamma[:, None, None] * jnp.maximum(rel, 0.0)), 0.0)
    xi = jnp.exp(log_gamma[:, None] * (idx + 1.0))[:, :, None]
    zeta = jnp.exp(log_gamma[:, None] * (c - 1.0 - idx))[:, :, None]
    gam = jnp.broadcast_to(jnp.exp(log_gamma * c)[:, None, None], (hh, 1, RET_HEAD_DIM))
    q_blk = 3 * DIFF_WIDTH // RET_WIDTH
    full3 = lambda shape: pl.BlockSpec(shape, lambda n: (0, 0, 0))
    return pl.pallas_call(
        _retention_kernel,
        grid=(s // c,),
        in_specs=[pl.BlockSpec((c, RET_WIDTH), lambda n: (n, q_blk)),
                  pl.BlockSpec((c, RET_WIDTH), lambda n: (n, q_blk + 1)),
                  pl.BlockSpec((c, RET_WIDTH), lambda n: (n, q_blk + 2)),
                  pl.BlockSpec((c, RET_WIDTH), lambda n: (n, 0)),
                  full3((hh, c, c)), full3((hh, c, 1)), full3((hh, c, 1)),
                  full3((hh, 1, RET_HEAD_DIM)),
                  pl.BlockSpec((1, RET_WIDTH), lambda n: (0, 0)),
                  pl.BlockSpec((1, RET_WIDTH), lambda n: (0, 0))],
        out_specs=pl.BlockSpec((c, RET_WIDTH), lambda n: (n, 0)),
        out_shape=jax.ShapeDtypeStruct((s, RET_WIDTH), BF16),
        scratch_shapes=[pltpu.VMEM((hh, RET_HEAD_DIM, RET_HEAD_DIM), F32)],
        compiler_params=_params("arbitrary"),
        name="retention",
    )(qkv, qkv, qkv, rg, decay, xi, zeta, gam, g, b)


def _outproj_kernel(a1_ref, a2_ref, w1_ref, w2_ref, x_ref, mod_ref, o_ref, *, gate_row):
    mix = (jnp.dot(a1_ref[...], w1_ref[...], preferred_element_type=F32)
           + jnp.dot(a2_ref[...], w2_ref[...], preferred_element_type=F32))
    gate = mod_ref[gate_row:gate_row + 1, :]
    o_ref[...] = DEEPNORM_ALPHA * x_ref[...] + gate * mix


def _outproj(a1, a2, w_bf16, x, mod):
    s, d = x.shape
    tm, tn = 512, 1024
    half = a1.shape[1]
    return pl.pallas_call(
        functools.partial(_outproj_kernel, gate_row=2),
        grid=(d // tn, s // tm),
        in_specs=[pl.BlockSpec((tm, half), lambda j, i: (i, 0)),
                  pl.BlockSpec((tm, half), lambda j, i: (i, 0)),
                  pl.BlockSpec((half, tn), lambda j, i: (0, j)),
                  pl.BlockSpec((half, tn), lambda j, i: (1, j)),
                  pl.BlockSpec((tm, tn), lambda j, i: (i, j)),
                  pl.BlockSpec((N_MOD, tn), lambda j, i: (0, j))],
        out_specs=pl.BlockSpec((tm, tn), lambda j, i: (i, j)),
        out_shape=jax.ShapeDtypeStruct((s, d), F32),
        compiler_params=_params("arbitrary", "arbitrary"),
        name="out_proj",
    )(a1, a2, w_bf16, w_bf16, x, mod)


def _layer_norm(y, g, b):
    mu = jnp.mean(y, axis=-1, keepdims=True)
    cen = y - mu
    var = jnp.mean(cen * cen, axis=-1, keepdims=True)
    return cen * lax.rsqrt(var + LN_EPS) * g + b


def _ln_mod_kernel(y_ref, g_ref, b_ref, mod_ref, x_ref, h_ref, *, shift_row, scale_row):
    xn = _layer_norm(y_ref[...], g_ref[...], b_ref[...])
    x_ref[...] = xn
    shift = mod_ref[shift_row:shift_row + 1, :]
    scale = mod_ref[scale_row:scale_row + 1, :]
    h_ref[...] = (xn * (1.0 + scale) + shift).astype(BF16)


def _ln_mod(y, g, b, mod):
    s, d = y.shape
    tm = 256
    row = pl.BlockSpec((tm, d), lambda i: (i, 0))
    vec = pl.BlockSpec((1, d), lambda i: (0, 0))
    return pl.pallas_call(
        functools.partial(_ln_mod_kernel, shift_row=3, scale_row=4),
        grid=(s // tm,),
        in_specs=[row, vec, vec, pl.BlockSpec((N_MOD, d), lambda i: (0, 0))],
        out_specs=[row, row],
        out_shape=[jax.ShapeDtypeStruct((s, d), F32), jax.ShapeDtypeStruct((s, d), BF16)],
        compiler_params=_params("arbitrary"),
        name="ln1_modulate2",
    )(y, g, b, mod)


def _ln_kernel(y_ref, g_ref, b_ref, o_ref):
    o_ref[...] = _layer_norm(y_ref[...], g_ref[...], b_ref[...])


def _ln(y, g, b):
    s, d = y.shape
    tm = 256
    row = pl.BlockSpec((tm, d), lambda i: (i, 0))
    vec = pl.BlockSpec((1, d), lambda i: (0, 0))
    return pl.pallas_call(
        _ln_kernel,
        grid=(s // tm,),
        in_specs=[row, vec, vec],
        out_specs=row,
        out_shape=jax.ShapeDtypeStruct((s, d), F32),
        compiler_params=_params("arbitrary"),
        name="ln2",
    )(y, g, b)


def _upproj_kernel(a_ref, wg_ref, wv_ref, cwg_ref, cwv_ref, cbg_ref, cbv_ref, o_ref,
                   u_ref, wb_ref, *, tm, tn):
    first = pl.program_id(1) == 0

    @pl.when(first)
    def _():
        wb_ref[:, 0:tn] = wg_ref[...].astype(BF16)
        wb_ref[:, tn:2 * tn] = wv_ref[...].astype(BF16)
        u_ref[0:8, :] = jnp.zeros((8, 2 * tn), F32)

    @pl.when(jnp.logical_not(first))
    def _():
        u_ref[0:8, :] = u_ref[tm:tm + 8, :]

    u_ref[8:tm + 8, :] = jnp.dot(a_ref[...], wb_ref[...], preferred_element_type=F32)
    cw = jnp.concatenate([cwg_ref[...], cwv_ref[...]], axis=1)
    cb = jnp.concatenate([cbg_ref[...], cbv_ref[...]], axis=1)
    y = (cw[0:1, :] * u_ref[6:tm + 6, :] + cw[1:2, :] * u_ref[7:tm + 7, :]
         + cw[2:3, :] * u_ref[8:tm + 8, :] + cb)
    o_ref[...] = (_silu(y[:, 0:tn]) * y[:, tn:2 * tn]).astype(BF16)


def _upproj(h, w, conv_w, conv_b, tm=1024, tn=FFN_TILE):
    s, k = h.shape
    ffn = w.shape[1] // 2
    nj = ffn // tn
    return pl.pallas_call(
        functools.partial(_upproj_kernel, tm=tm, tn=tn),
        grid=(nj, s // tm),
        in_specs=[pl.BlockSpec((tm, k), lambda j, i: (i, 0)),
                  pl.BlockSpec((k, tn), lambda j, i: (0, j)),
                  pl.BlockSpec((k, tn), lambda j, i: (0, nj + j)),
                  pl.BlockSpec((CONV_WIDTH, tn), lambda j, i: (0, j)),
                  pl.BlockSpec((CONV_WIDTH, tn), lambda j, i: (0, nj + j)),
                  pl.BlockSpec((1, tn), lambda j, i: (0, j)),
                  pl.BlockSpec((1, tn), lambda j, i: (0, nj + j))],
        out_specs=pl.BlockSpec((tm, tn), lambda j, i: (i, j)),
        out_shape=jax.ShapeDtypeStruct((s, ffn), BF16),
        scratch_shapes=[pltpu.VMEM((tm + 8, 2 * tn), F32), pltpu.VMEM((k, 2 * tn), BF16)],
        compiler_params=_params("arbitrary", "arbitrary"),
        name="up_proj_conv_gate",
    )(h, w, w, conv_w, conv_w, conv_b, conv_b)


def _downproj_kernel(a_ref, w_ref, x_ref, mod_ref, o_ref, *, gate_row):
    f = jnp.dot(a_ref[...], w_ref[...], preferred_element_type=F32)
    gate = mod_ref[gate_row:gate_row + 1, :]
    o_ref[...] = DEEPNORM_ALPHA * x_ref[...] + gate * f


def _downproj(a, w_bf16, x, mod):
    s, d = x.shape
    k = a.shape[1]
    tm, tn = 512, 512
    return pl.pallas_call(
        functools.partial(_downproj_kernel, gate_row=5),
        grid=(d // tn, s // tm),
        in_specs=[pl.BlockSpec((tm, k), lambda j, i: (i, 0)),
                  pl.BlockSpec((k, tn), lambda j, i: (0, j)),
                  pl.BlockSpec((tm, tn), lambda j, i: (i, j)),
                  pl.BlockSpec((N_MOD, tn), lambda j, i: (0, j))],
        out_specs=pl.BlockSpec((tm, tn), lambda j, i: (i, j)),
        out_shape=jax.ShapeDtypeStruct((s, d), F32),
        compiler_params=_params("arbitrary", "arbitrary"),
        name="down_proj",
    )(a, w_bf16, x, mod)


def _rotary_tables(seq, d):
    inv_freq = ROPE_BASE ** (-jnp.arange(0, d, 2, dtype=F32) / d)
    ang = jnp.arange(seq, dtype=F32)[:, None] * inv_freq[None, :]
    return jnp.cos(ang), jnp.sin(ang)


def kernel(x, c, w_ada, b_ada, w_in, lambda_q1, lambda_k1, lambda_q2, lambda_k2, diff_norm_g,
           ret_norm_g, ret_norm_b, w_out, ln1_g, ln1_b, w_up, conv_w, conv_b, w_down, ln2_g, ln2_b):
    batch, seq, d = x.shape
    assert (batch, seq, d) == (1, SEQ, D_MODEL) and w_ada.shape[0] == DEPTH
    l = 0
    x2 = x.reshape(seq, d)
    cos, sin = _rotary_tables(seq, RET_HEAD_DIM)

    w_out_b = w_out[l].astype(BF16)
    w_down_b = w_down[l].astype(BF16)

    mod = _ada(c.reshape(d, 1), w_ada[l], b_ada[l][None, :]).reshape(N_MOD, d)

    h1 = _modulate(x2, mod, shift_row=0, scale_row=1)
    qkv = _inproj(h1, w_in[l], cos, sin)
    rg = _gateproj(h1, w_in[l])
    a_out = _diff_attention(qkv, diff_norm_g[l][:, None], lambda_q1[l][None, :], lambda_k1[l][None, :],
                            lambda_q2[l][None, :], lambda_k2[l][None, :])
    r_out = _retention(qkv, rg, ret_norm_g[l][None, :], ret_norm_b[l][None, :])
    y1 = _outproj(a_out, r_out, w_out_b, x2, mod)
    x1, h2 = _ln_mod(y1, ln1_g[l][None, :], ln1_b[l][None, :], mod)

    gated = _upproj(h2, w_up[l], conv_w[l], conv_b[l][None, :])
    y2 = _downproj(gated, w_down_b, x1, mod)
    out = _ln(y2, ln2_g[l][None, :], ln2_b[l][None, :])
    return out.reshape(batch, seq, d)
```

```python
import functools
import math

import jax
import jax.numpy as jnp
from jax import lax
from jax.experimental import pallas as pl
from jax.experimental.pallas import tpu as pltpu

F32 = jnp.float32
BF16 = jnp.bfloat16

D_MODEL = 4096
SEQ = 8192
DIFF_WIDTH = D_MODEL // 2
RET_WIDTH = D_MODEL - DIFF_WIDTH
DIFF_HEADS = 8
DIFF_HEAD_DIM = DIFF_WIDTH // (2 * DIFF_HEADS)
DIFF_V_DIM = 2 * DIFF_HEAD_DIM
RET_HEADS = 8
RET_HEAD_DIM = RET_WIDTH // RET_HEADS
FFN_DIM = ((8 * D_MODEL // 3 + 255) // 256) * 256
CONV_WIDTH = 3
RET_CHUNK = 128
ROPE_BASE = 10000.0
LN_EPS = 1e-5
N_MOD = 6
DEPTH = 1
DEEPNORM_ALPHA = (2.0 * DEPTH) ** 0.25
LAM_INIT = 0.8 - 0.6 * math.exp(-0.3 * 0)

V7X_VMEM_BYTES = 64 * 1024 * 1024
VMEM_LIMIT = V7X_VMEM_BYTES - 8 * 1024 * 1024

HEAD_COLS = 256
FFN_TILE = 256
MASK_VALUE = -1e30


def _params(*sem):
    return pltpu.CompilerParams(dimension_semantics=sem, vmem_limit_bytes=VMEM_LIMIT)


def _silu(v):
    return v / (1.0 + jnp.exp(-v))


def _ada_kernel(c_ref, w_ref, b_ref, o_ref):
    c = c_ref[...]
    o_ref[...] = jnp.sum(w_ref[...] * _silu(c), axis=0, keepdims=True) + b_ref[...]


def _ada(c_col, w_ada, b_ada):
    d, n = w_ada.shape
    tn = 512
    return pl.pallas_call(
        _ada_kernel,
        grid=(n // tn,),
        in_specs=[pl.BlockSpec((d, 1), lambda j: (0, 0)),
                  pl.BlockSpec((d, tn), lambda j: (0, j)),
                  pl.BlockSpec((1, tn), lambda j: (0, j))],
        out_specs=pl.BlockSpec((1, tn), lambda j: (0, j)),
        out_shape=jax.ShapeDtypeStruct((1, n), F32),
        compiler_params=_params("arbitrary"),
        name="ada_mod",
    )(c_col, w_ada, b_ada)


def _modulate_kernel(x_ref, mod_ref, o_ref, *, shift_row, scale_row):
    shift = mod_ref[shift_row:shift_row + 1, :]
    scale = mod_ref[scale_row:scale_row + 1, :]
    o_ref[...] = (x_ref[...] * (1.0 + scale) + shift).astype(BF16)


def _modulate(x, mod, shift_row, scale_row):
    s, d = x.shape
    tm = 512
    return pl.pallas_call(
        functools.partial(_modulate_kernel, shift_row=shift_row, scale_row=scale_row),
        grid=(s // tm,),
        in_specs=[pl.BlockSpec((tm, d), lambda i: (i, 0)),
                  pl.BlockSpec((N_MOD, d), lambda i: (0, 0))],
        out_specs=pl.BlockSpec((tm, d), lambda i: (i, 0)),
        out_shape=jax.ShapeDtypeStruct((s, d), BF16),
        compiler_params=_params("arbitrary"),
        name="modulate1",
    )(x, mod)


def _cast_weight_tile(w_ref, wb_ref):
    @pl.when(pl.program_id(1) == 0)
    def _():
        wb_ref[...] = w_ref[...].astype(BF16)


def _inproj_kernel(a_ref, w_ref, cos_ref, sin_ref, o_ref, wb_ref, *, tn, tm, nsplit):
    j = pl.program_id(0)
    per_group = DIFF_WIDTH // tn
    _cast_weight_tile(w_ref, wb_ref)
    is_rot = jnp.logical_and(j >= 3 * per_group, j < 5 * per_group)
    qscale = jnp.where(j < per_group, DIFF_HEAD_DIM ** -0.5 * math.log2(math.e), 1.0).astype(F32)
    kscale = jnp.where(j >= 4 * per_group, RET_HEAD_DIM ** -0.5, 1.0).astype(F32)
    half = RET_HEAD_DIM // 2
    rm = tm // nsplit

    def rows(r):
        return slice(r * rm, (r + 1) * rm)

    @pl.when(jnp.logical_not(is_rot))
    def _():
        for r in range(nsplit):
            acc = jnp.dot(a_ref[rows(r), :], wb_ref[...], preferred_element_type=F32)
            o_ref[rows(r), :] = (acc * qscale).astype(BF16)

    @pl.when(is_rot)
    def _():
        for r in range(nsplit):
            acc = jnp.dot(a_ref[rows(r), :], wb_ref[...], preferred_element_type=F32)
            cos = cos_ref[rows(r), :] * kscale
            sin = sin_ref[rows(r), :] * kscale
            for h in range(tn // RET_HEAD_DIM):
                lo = h * RET_HEAD_DIM
                x1 = acc[:, lo:lo + half]
                x2 = acc[:, lo + half:lo + 2 * half]
                o_ref[rows(r), lo:lo + half] = (x1 * cos - x2 * sin).astype(BF16)
                o_ref[rows(r), lo + half:lo + 2 * half] = (x2 * cos + x1 * sin).astype(BF16)


def _inproj(h, w, cos, sin):
    s, k = h.shape
    n = 3 * DIFF_WIDTH + 3 * RET_WIDTH
    tm, tn = 1024, 512
    return pl.pallas_call(
        functools.partial(_inproj_kernel, tn=tn, tm=tm, nsplit=2),
        grid=(n // tn, s // tm),
        in_specs=[pl.BlockSpec((tm, k), lambda j, i: (i, 0)),
                  pl.BlockSpec((k, tn), lambda j, i: (0, j)),
                  pl.BlockSpec((tm, RET_HEAD_DIM // 2), lambda j, i: (i, 0)),
                  pl.BlockSpec((tm, RET_HEAD_DIM // 2), lambda j, i: (i, 0))],
        out_specs=pl.BlockSpec((tm, tn), lambda j, i: (i, j)),
        out_shape=jax.ShapeDtypeStruct((s, n), BF16),
        scratch_shapes=[pltpu.VMEM((k, tn), BF16)],
        compiler_params=_params("arbitrary", "arbitrary"),
        name="in_proj",
    )(h, w, cos, sin)


def _gateproj_kernel(a_ref, w_ref, o_ref, wb_ref, *, tm, nsplit):
    _cast_weight_tile(w_ref, wb_ref)
    rm = tm // nsplit
    for r in range(nsplit):
        rows = slice(r * rm, (r + 1) * rm)
        o_ref[rows, :] = jnp.dot(a_ref[rows, :], wb_ref[...], preferred_element_type=F32)


def _gateproj(h, w):
    s, k = h.shape
    tm, tn = 1024, 512
    col0 = (3 * DIFF_WIDTH + 3 * RET_WIDTH) // tn
    return pl.pallas_call(
        functools.partial(_gateproj_kernel, tm=tm, nsplit=2),
        grid=(RET_WIDTH // tn, s // tm),
        in_specs=[pl.BlockSpec((tm, k), lambda j, i: (i, 0)),
                  pl.BlockSpec((k, tn), lambda j, i: (0, col0 + j))],
        out_specs=pl.BlockSpec((tm, tn), lambda j, i: (i, j)),
        out_shape=jax.ShapeDtypeStruct((s, RET_WIDTH), F32),
        scratch_shapes=[pltpu.VMEM((k, tn), BF16)],
        compiler_params=_params("arbitrary", "arbitrary"),
        name="gate_proj",
    )(h, w)


def _attn_kernel(q_ref, k_ref, v_ref, g_ref, lq1_ref, lk1_ref, lq2_ref, lk2_ref, o_ref, vt_ref,
                 *, tq, tk):
    i = pl.program_id(1)
    d = DIFF_HEAD_DIM
    nt = (((1,), (1,)), ((), ()))
    q = (q_ref[:, 0:d], q_ref[:, d:2 * d])

    @pl.when(i == 0)
    def _():
        vt_ref[...] = v_ref[...].T

    def scores(j):
        start = pl.multiple_of(j * tk, tk)
        kb = k_ref[pl.ds(start, tk), :]
        sp = tuple(lax.dot_general(kb[:, m * d:(m + 1) * d], q[m], nt, preferred_element_type=F32)
                   for m in range(2))
        return sp, tuple(jnp.max(x, axis=0, keepdims=True) for x in sp)

    def consume(j, s_pair, carry, masked):
        start = pl.multiple_of(j * tk, tk)
        vtb = vt_ref[:, pl.ds(start, tk)]
        new = []
        for m in range(2):
            m_old, l_old, acc_old = carry[m]
            s = s_pair[0][m]
            if masked:
                key = lax.broadcasted_iota(jnp.int32, s.shape, 0)
                qry = lax.broadcasted_iota(jnp.int32, s.shape, 1)
                s = jnp.where(key - qry <= i * tq - j * tk, s, MASK_VALUE)
                m_new = jnp.maximum(m_old, jnp.max(s, axis=0, keepdims=True))
            else:
                m_new = jnp.maximum(m_old, s_pair[1][m])
            alpha = jnp.exp2(m_old - m_new)
            p = jnp.exp2(s - m_new)
            l_new = alpha * l_old + jnp.sum(p, axis=0, keepdims=True)
            acc_new = alpha * acc_old + jnp.dot(vtb, p.astype(BF16), preferred_element_type=F32)
            new.append((m_new, l_new, acc_new))
        return tuple(new)

    init_one = (jnp.full((1, tq), MASK_VALUE, F32), jnp.zeros((1, tq), F32),
                jnp.zeros((DIFF_V_DIM, tq), F32))
    n_full = (i * tq) // tk
    n_diag = max(1, tq // tk)

    def body(j, carry):
        return consume(j, scores(j), carry, False)

    carry = lax.fori_loop(0, n_full, body, (init_one, init_one))
    for dd in range(n_diag):
        carry = consume(n_full + dd, scores(n_full + dd), carry, True)
    (_, l0, acc0), (_, l1, acc1) = carry

    lam = (jnp.exp(jnp.sum(lq1_ref[...] * lk1_ref[...], axis=-1, keepdims=True))
           - jnp.exp(jnp.sum(lq2_ref[...] * lk2_ref[...], axis=-1, keepdims=True)) + LAM_INIT)
    out = acc0 / l0 - lam * (acc1 / l1)
    y = out * lax.rsqrt(jnp.mean(out * out, axis=0, keepdims=True) + LN_EPS)
    o_ref[...] = (y * g_ref[...] * (1.0 - LAM_INIT)).T.astype(BF16)


def _diff_attention(qkv, g_col, lq1, lk1, lq2, lk2):
    s = qkv.shape[0]
    tq, tk = 1024, 1024
    k_col0 = DIFF_WIDTH // HEAD_COLS
    v_col0 = 2 * DIFF_WIDTH // HEAD_COLS
    vec = pl.BlockSpec((1, DIFF_HEAD_DIM), lambda h, i: (0, 0))
    return pl.pallas_call(
        functools.partial(_attn_kernel, tq=tq, tk=tk),
        grid=(DIFF_HEADS, s // tq),
        in_specs=[pl.BlockSpec((tq, HEAD_COLS), lambda h, i: (i, h)),
                  pl.BlockSpec((s, HEAD_COLS), lambda h, i: (0, k_col0 + h)),
                  pl.BlockSpec((s, HEAD_COLS), lambda h, i: (0, v_col0 + h)),
                  pl.BlockSpec((DIFF_V_DIM, 1), lambda h, i: (0, 0)),
                  vec, vec, vec, vec],
        out_specs=pl.BlockSpec((tq, DIFF_V_DIM), lambda h, i: (i, h)),
        out_shape=jax.ShapeDtypeStruct((s, DIFF_WIDTH), BF16),
        scratch_shapes=[pltpu.VMEM((DIFF_V_DIM, s), BF16)],
        compiler_params=_params("arbitrary", "arbitrary"),
        name="diff_attn",
    )(qkv, qkv, qkv, g_col, lq1, lk1, lq2, lk2)


def _retention_kernel(q_ref, k_ref, v_ref, rg_ref, decay_ref, xi_ref, zeta_ref, gam_ref,
                      g_ref, b_ref, o_ref, state_ref):
    @pl.when(pl.program_id(0) == 0)
    def _():
        state_ref[...] = jnp.zeros_like(state_ref)

    dh = RET_HEAD_DIM
    for h in range(RET_HEADS):
        cols = slice(h * dh, (h + 1) * dh)
        qc = q_ref[:, cols]
        kc = k_ref[:, cols]
        vc = v_ref[:, cols]
        st = state_ref[h]
        inner = lax.dot_general(qc, kc, (((1,), (1,)), ((), ())),
                                preferred_element_type=F32) * decay_ref[h]
        o = (jnp.dot(inner.astype(BF16), vc, preferred_element_type=F32)
             + jnp.dot(qc, st.astype(BF16), preferred_element_type=F32) * xi_ref[h])
        kz = (kc.astype(F32) * zeta_ref[h]).astype(BF16)
        state_ref[h] = st * gam_ref[h] + lax.dot_general(
            kz, vc, (((0,), (0,)), ((), ())), preferred_element_type=F32)
        mu = jnp.mean(o, axis=-1, keepdims=True)
        cen = o - mu
        var = jnp.mean(cen * cen, axis=-1, keepdims=True)
        y = cen * lax.rsqrt(var + LN_EPS) * g_ref[:, cols] + b_ref[:, cols]
        o_ref[:, cols] = (_silu(rg_ref[:, cols]) * y).astype(BF16)


def _retention(qkv, rg, g, b):
    s = qkv.shape[0]
    c = RET_CHUNK
    hh = RET_HEADS
    log_gamma = jnp.log1p(-jnp.exp2(-5.0 - jnp.arange(hh, dtype=F32)))
    idx = jnp.arange(c, dtype=F32)
    rel = idx[:, None] - idx[None, :]
    decay = jnp.where(rel >= 0, jnp.exp(log_gamma[:, None, None] * jnp.maximum(rel, 0.0)), 0.0)
    xi = jnp.exp(log_gamma[:, None] * (idx + 1.0))[:, :, None]
    zeta = jnp.exp(log_gamma[:, None] * (c - 1.0 - idx))[:, :, None]
    gam = jnp.broadcast_to(jnp.exp(log_gamma * c)[:, None, None], (hh, 1, RET_HEAD_DIM))
    q_blk = 3 * DIFF_WIDTH // RET_WIDTH
    full3 = lambda shape: pl.BlockSpec(shape, lambda n: (0, 0, 0))
    return pl.pallas_call(
        _retention_kernel,
        grid=(s // c,),
        in_specs=[pl.BlockSpec((c, RET_WIDTH), lambda n: (n, q_blk)),
                  pl.BlockSpec((c, RET_WIDTH), lambda n: (n, q_blk + 1)),
                  pl.BlockSpec((c, RET_WIDTH), lambda n: (n, q_blk + 2)),
                  pl.BlockSpec((c, RET_WIDTH), lambda n: (n, 0)),
                  full3((hh, c, c)), full3((hh, c, 1)), full3((hh, c, 1)),
                  full3((hh, 1, RET_HEAD_DIM)),
                  pl.BlockSpec((1, RET_WIDTH), lambda n: (0, 0)),
                  pl.BlockSpec((1, RET_WIDTH), lambda n: (0, 0))],
        out_specs=pl.BlockSpec((c, RET_WIDTH), lambda n: (n, 0)),
        out_shape=jax.ShapeDtypeStruct((s, RET_WIDTH), BF16),
        scratch_shapes=[pltpu.VMEM((hh, RET_HEAD_DIM, RET_HEAD_DIM), F32)],
        compiler_params=_params("arbitrary"),
        name="retention",
    )(qkv, qkv, qkv, rg, decay, xi, zeta, gam, g, b)


def _outproj_kernel(a1_ref, a2_ref, w1_ref, w2_ref, x_ref, mod_ref, o_ref, w1b_ref, w2b_ref,
                    *, gate_row, tm, nsplit):
    _cast_weight_tile(w1_ref, w1b_ref)
    _cast_weight_tile(w2_ref, w2b_ref)
    gate = mod_ref[gate_row:gate_row + 1, :]
    rm = tm // nsplit
    for r in range(nsplit):
        rows = slice(r * rm, (r + 1) * rm)
        mix = (jnp.dot(a1_ref[rows, :], w1b_ref[...], preferred_element_type=F32)
               + jnp.dot(a2_ref[rows, :], w2b_ref[...], preferred_element_type=F32))
        o_ref[rows, :] = DEEPNORM_ALPHA * x_ref[rows, :] + gate * mix


def _outproj(a1, a2, w, x, mod):
    s, d = x.shape
    tm, tn = 1024, 512
    half = a1.shape[1]
    return pl.pallas_call(
        functools.partial(_outproj_kernel, gate_row=2, tm=tm, nsplit=2),
        grid=(d // tn, s // tm),
        in_specs=[pl.BlockSpec((tm, half), lambda j, i: (i, 0)),
                  pl.BlockSpec((tm, half), lambda j, i: (i, 0)),
                  pl.BlockSpec((half, tn), lambda j, i: (0, j)),
                  pl.BlockSpec((half, tn), lambda j, i: (1, j)),
                  pl.BlockSpec((tm, tn), lambda j, i: (i, j)),
                  pl.BlockSpec((N_MOD, tn), lambda j, i: (0, j))],
        out_specs=pl.BlockSpec((tm, tn), lambda j, i: (i, j)),
        out_shape=jax.ShapeDtypeStruct((s, d), F32),
        scratch_shapes=[pltpu.VMEM((half, tn), BF16), pltpu.VMEM((half, tn), BF16)],
        compiler_params=_params("arbitrary", "arbitrary"),
        name="out_proj",
    )(a1, a2, w, w, x, mod)


def _layer_norm(y, g, b):
    mu = jnp.mean(y, axis=-1, keepdims=True)
    cen = y - mu
    var = jnp.mean(cen * cen, axis=-1, keepdims=True)
    return cen * lax.rsqrt(var + LN_EPS) * g + b


def _ln_mod_kernel(y_ref, g_ref, b_ref, mod_ref, x_ref, h_ref, *, shift_row, scale_row):
    xn = _layer_norm(y_ref[...], g_ref[...], b_ref[...])
    x_ref[...] = xn
    shift = mod_ref[shift_row:shift_row + 1, :]
    scale = mod_ref[scale_row:scale_row + 1, :]
    h_ref[...] = (xn * (1.0 + scale) + shift).astype(BF16)


def _ln_mod(y, g, b, mod):
    s, d = y.shape
    tm = 512
    row = pl.BlockSpec((tm, d), lambda i: (i, 0))
    vec = pl.BlockSpec((1, d), lambda i: (0, 0))
    return pl.pallas_call(
        functools.partial(_ln_mod_kernel, shift_row=3, scale_row=4),
        grid=(s // tm,),
        in_specs=[row, vec, vec, pl.BlockSpec((N_MOD, d), lambda i: (0, 0))],
        out_specs=[row, row],
        out_shape=[jax.ShapeDtypeStruct((s, d), F32), jax.ShapeDtypeStruct((s, d), BF16)],
        compiler_params=_params("arbitrary"),
        name="ln1_modulate2",
    )(y, g, b, mod)


def _ln_kernel(y_ref, g_ref, b_ref, o_ref):
    o_ref[...] = _layer_norm(y_ref[...], g_ref[...], b_ref[...])


def _ln(y, g, b):
    s, d = y.shape
    tm = 512
    row = pl.BlockSpec((tm, d), lambda i: (i, 0))
    vec = pl.BlockSpec((1, d), lambda i: (0, 0))
    return pl.pallas_call(
        _ln_kernel,
        grid=(s // tm,),
        in_specs=[row, vec, vec],
        out_specs=row,
        out_shape=jax.ShapeDtypeStruct((s, d), F32),
        compiler_params=_params("arbitrary"),
        name="ln2",
    )(y, g, b)


def _upproj_kernel(a_ref, wg_ref, wv_ref, cwg_ref, cwv_ref, cbg_ref, cbv_ref, o_ref,
                   u_ref, wb_ref, *, tm, tn):
    first = pl.program_id(1) == 0

    @pl.when(first)
    def _():
        wb_ref[:, 0:tn] = wg_ref[...].astype(BF16)
        wb_ref[:, tn:2 * tn] = wv_ref[...].astype(BF16)
        u_ref[0:8, :] = jnp.zeros((8, 2 * tn), F32)

    @pl.when(jnp.logical_not(first))
    def _():
        u_ref[0:8, :] = u_ref[tm:tm + 8, :]

    u_ref[8:tm + 8, :] = jnp.dot(a_ref[...], wb_ref[...], preferred_element_type=F32)
    cw = jnp.concatenate([cwg_ref[...], cwv_ref[...]], axis=1)
    cb = jnp.concatenate([cbg_ref[...], cbv_ref[...]], axis=1)
    y = (cw[0:1, :] * u_ref[6:tm + 6, :] + cw[1:2, :] * u_ref[7:tm + 7, :]
         + cw[2:3, :] * u_ref[8:tm + 8, :] + cb)
    o_ref[...] = (_silu(y[:, 0:tn]) * y[:, tn:2 * tn]).astype(BF16)


def _upproj(h, w, conv_w, conv_b, tm=1024, tn=FFN_TILE):
    s, k = h.shape
    ffn = w.shape[1] // 2
    nj = ffn // tn
    return pl.pallas_call(
        functools.partial(_upproj_kernel, tm=tm, tn=tn),
        grid=(nj, s // tm),
        in_specs=[pl.BlockSpec((tm, k), lambda j, i: (i, 0)),
                  pl.BlockSpec((k, tn), lambda j, i: (0, j)),
                  pl.BlockSpec((k, tn), lambda j, i: (0, nj + j)),
                  pl.BlockSpec((CONV_WIDTH, tn), lambda j, i: (0, j)),
                  pl.BlockSpec((CONV_WIDTH, tn), lambda j, i: (0, nj + j)),
                  pl.BlockSpec((1, tn), lambda j, i: (0, j)),
                  pl.BlockSpec((1, tn), lambda j, i: (0, nj + j))],
        out_specs=pl.BlockSpec((tm, tn), lambda j, i: (i, j)),
        out_shape=jax.ShapeDtypeStruct((s, ffn), BF16),
        scratch_shapes=[pltpu.VMEM((tm + 8, 2 * tn), F32), pltpu.VMEM((k, 2 * tn), BF16)],
        compiler_params=_params("arbitrary", "arbitrary"),
        name="up_proj_conv_gate",
    )(h, w, w, conv_w, conv_w, conv_b, conv_b)


def _downproj_kernel(a_ref, w_ref, x_ref, mod_ref, o_ref, *, gate_row):
    f = jnp.dot(a_ref[...], w_ref[...], preferred_element_type=F32)
    gate = mod_ref[gate_row:gate_row + 1, :]
    o_ref[...] = DEEPNORM_ALPHA * x_ref[...] + gate * f


def _downproj(a, w_bf16, x, mod):
    s, d = x.shape
    k = a.shape[1]
    tm, tn = 512, 512
    return pl.pallas_call(
        functools.partial(_downproj_kernel, gate_row=5),
        grid=(d // tn, s // tm),
        in_specs=[pl.BlockSpec((tm, k), lambda j, i: (i, 0)),
                  pl.BlockSpec((k, tn), lambda j, i: (0, j)),
                  pl.BlockSpec((tm, tn), lambda j, i: (i, j)),
                  pl.BlockSpec((N_MOD, tn), lambda j, i: (0, j))],
        out_specs=pl.BlockSpec((tm, tn), lambda j, i: (i, j)),
        out_shape=jax.ShapeDtypeStruct((s, d), F32),
        compiler_params=_params("arbitrary", "arbitrary"),
        name="down_proj",
    )(a, w_bf16, x, mod)


def _rotary_tables(seq, d):
    inv_freq = ROPE_BASE ** (-jnp.arange(0, d, 2, dtype=F32) / d)
    ang = jnp.arange(seq, dtype=F32)[:, None] * inv_freq[None, :]
    return jnp.cos(ang), jnp.sin(ang)


def kernel(x, c, w_ada, b_ada, w_in, lambda_q1, lambda_k1, lambda_q2, lambda_k2, diff_norm_g,
           ret_norm_g, ret_norm_b, w_out, ln1_g, ln1_b, w_up, conv_w, conv_b, w_down, ln2_g, ln2_b):
    batch, seq, d = x.shape
    assert (batch, seq, d) == (1, SEQ, D_MODEL) and w_ada.shape[0] == DEPTH
    l = 0
    x2 = x.reshape(seq, d)
    cos, sin = _rotary_tables(seq, RET_HEAD_DIM)

    w_down_b = w_down[l].astype(BF16)

    mod = _ada(c.reshape(d, 1), w_ada[l], b_ada[l][None, :]).reshape(N_MOD, d)

    h1 = _modulate(x2, mod, shift_row=0, scale_row=1)
    qkv = _inproj(h1, w_in[l], cos, sin)
    rg = _gateproj(h1, w_in[l])
    a_out = _diff_attention(qkv, diff_norm_g[l][:, None], lambda_q1[l][None, :], lambda_k1[l][None, :],
                            lambda_q2[l][None, :], lambda_k2[l][None, :])
    r_out = _retention(qkv, rg, ret_norm_g[l][None, :], ret_norm_b[l][None, :])
    y1 = _outproj(a_out, r_out, w_out[l], x2, mod)
    x1, h2 = _ln_mod(y1, ln1_g[l][None, :], ln1_b[l][None, :], mod)

    gated = _upproj(h2, w_up[l], conv_w[l], conv_b[l][None, :])
    y2 = _downproj(gated, w_down_b, x1, mod)
    out = _ln(y2, ln2_g[l][None, :], ln2_b[l][None, :])
    return out.reshape(batch, seq, d)
```

```python
import functools
import math

import jax
import jax.numpy as jnp
from jax import lax
from jax.experimental import pallas as pl
from jax.experimental.pallas import tpu as pltpu

F32 = jnp.float32
BF16 = jnp.bfloat16

D_MODEL = 4096
SEQ = 8192
DIFF_WIDTH = D_MODEL // 2
RET_WIDTH = D_MODEL - DIFF_WIDTH
DIFF_HEADS = 8
DIFF_HEAD_DIM = DIFF_WIDTH // (2 * DIFF_HEADS)
DIFF_V_DIM = 2 * DIFF_HEAD_DIM
RET_HEADS = 8
RET_HEAD_DIM = RET_WIDTH // RET_HEADS
FFN_DIM = ((8 * D_MODEL // 3 + 255) // 256) * 256
CONV_WIDTH = 3
RET_CHUNK = 128
ROPE_BASE = 10000.0
LN_EPS = 1e-5
N_MOD = 6
DEPTH = 1
DEEPNORM_ALPHA = (2.0 * DEPTH) ** 0.25
LAM_INIT = 0.8 - 0.6 * math.exp(-0.3 * 0)

V7X_VMEM_BYTES = 64 * 1024 * 1024
VMEM_LIMIT = V7X_VMEM_BYTES - 8 * 1024 * 1024

HEAD_COLS = 256
FFN_TILE = 256
MASK_VALUE = -1e30


def _params(*sem):
    return pltpu.CompilerParams(dimension_semantics=sem, vmem_limit_bytes=VMEM_LIMIT)


def _silu(v):
    return v / (1.0 + jnp.exp(-v))


ADA_LANES = 128


def _ada_kernel(c_ref, w_ref, b_ref, o_ref, cact_ref):
    cact = _silu(c_ref[...])
    o_ref[...] = jnp.sum(w_ref[...] * cact, axis=0, keepdims=True) + b_ref[...]
    cact_ref[...] = jnp.broadcast_to(cact, cact_ref.shape)


def _ada_head(c_col, w_ada, b_ada, n_cols):
    d = w_ada.shape[0]
    tn = 512
    return pl.pallas_call(
        _ada_kernel,
        grid=(n_cols // tn,),
        in_specs=[pl.BlockSpec((d, 1), lambda j: (0, 0)),
                  pl.BlockSpec((d, tn), lambda j: (0, j)),
                  pl.BlockSpec((1, tn), lambda j: (0, j))],
        out_specs=[pl.BlockSpec((1, tn), lambda j: (0, j)),
                   pl.BlockSpec((d, ADA_LANES), lambda j: (0, 0))],
        out_shape=[jax.ShapeDtypeStruct((1, n_cols), F32),
                   jax.ShapeDtypeStruct((d, ADA_LANES), F32)],
        compiler_params=_params("arbitrary"),
        name="ada_mod",
    )(c_col, w_ada, b_ada)


def _modulate_kernel(x_ref, mod_ref, o_ref, *, shift_row, scale_row):
    shift = mod_ref[shift_row:shift_row + 1, :]
    scale = mod_ref[scale_row:scale_row + 1, :]
    o_ref[...] = (x_ref[...] * (1.0 + scale) + shift).astype(BF16)


def _modulate(x, mod, shift_row, scale_row):
    s, d = x.shape
    tm = 512
    return pl.pallas_call(
        functools.partial(_modulate_kernel, shift_row=shift_row, scale_row=scale_row),
        grid=(s // tm,),
        in_specs=[pl.BlockSpec((tm, d), lambda i: (i, 0)),
                  pl.BlockSpec(mod.shape, lambda i: (0, 0))],
        out_specs=pl.BlockSpec((tm, d), lambda i: (i, 0)),
        out_shape=jax.ShapeDtypeStruct((s, d), BF16),
        compiler_params=_params("arbitrary"),
        name="modulate1",
    )(x, mod)


def _cast_weight_tile(w_ref, wb_ref):
    @pl.when(pl.program_id(1) == 0)
    def _():
        wb_ref[...] = w_ref[...].astype(BF16)


def _inproj_kernel(a_ref, w_ref, cos_ref, sin_ref, cact_ref, wada_ref, bada_ref, o_ref, mod_ref,
                   wb_ref, *, tn, tm, nsplit):
    j = pl.program_id(0)
    per_group = DIFF_WIDTH // tn
    _cast_weight_tile(w_ref, wb_ref)

    def ada_slab():
        groups = 16
        rws = wada_ref.shape[0] // groups
        parts = [jnp.sum(wada_ref[g * rws:(g + 1) * rws, :] * cact_ref[g * rws:(g + 1) * rws, :],
                         axis=0, keepdims=True) for g in range(groups)]
        while len(parts) > 1:
            parts = [parts[p] + parts[p + 1] for p in range(0, len(parts), 2)]
        mod_ref[...] = parts[0] + bada_ref[...]

    is_rot = jnp.logical_and(j >= 3 * per_group, j < 5 * per_group)
    qscale = jnp.where(j < per_group, DIFF_HEAD_DIM ** -0.5 * math.log2(math.e), 1.0).astype(F32)
    kscale = jnp.where(j >= 4 * per_group, RET_HEAD_DIM ** -0.5, 1.0).astype(F32)
    half = RET_HEAD_DIM // 2
    rm = tm // nsplit

    def rows(r):
        return slice(r * rm, (r + 1) * rm)

    @pl.when(jnp.logical_not(is_rot))
    def _():
        for r in range(nsplit):
            acc = jnp.dot(a_ref[rows(r), :], wb_ref[...], preferred_element_type=F32)
            o_ref[rows(r), :] = (acc * qscale).astype(BF16)
            if r == 0:
                ada_slab()

    @pl.when(is_rot)
    def _():
        for r in range(nsplit):
            acc = jnp.dot(a_ref[rows(r), :], wb_ref[...], preferred_element_type=F32)
            cos = cos_ref[rows(r), :] * kscale
            sin = sin_ref[rows(r), :] * kscale
            for h in range(tn // RET_HEAD_DIM):
                lo = h * RET_HEAD_DIM
                x1 = acc[:, lo:lo + half]
                x2 = acc[:, lo + half:lo + 2 * half]
                o_ref[rows(r), lo:lo + half] = (x1 * cos - x2 * sin).astype(BF16)
                o_ref[rows(r), lo + half:lo + 2 * half] = (x2 * cos + x1 * sin).astype(BF16)
            if r == 0:
                ada_slab()


def _inproj(h, w, cos, sin, cact, w_ada, b_ada, ada_col0):
    s, k = h.shape
    n = 3 * DIFF_WIDTH + 3 * RET_WIDTH
    tm, tn = 1024, 512
    ni = s // tm
    n_ada = w_ada.shape[1] - ada_col0
    ada_steps = n_ada // ADA_LANES
    assert ada_steps * ADA_LANES == n_ada and ada_steps <= (n // tn) * ni
    blk0 = ada_col0 // ADA_LANES

    def ada_blk(j, i):
        return jnp.minimum(j * ni + i, ada_steps - 1)

    return pl.pallas_call(
        functools.partial(_inproj_kernel, tn=tn, tm=tm, nsplit=2),
        grid=(n // tn, ni),
        in_specs=[pl.BlockSpec((tm, k), lambda j, i: (i, 0)),
                  pl.BlockSpec((k, tn), lambda j, i: (0, j)),
                  pl.BlockSpec((tm, RET_HEAD_DIM // 2), lambda j, i: (i, 0)),
                  pl.BlockSpec((tm, RET_HEAD_DIM // 2), lambda j, i: (i, 0)),
                  pl.BlockSpec((k, ADA_LANES), lambda j, i: (0, 0)),
                  pl.BlockSpec((k, ADA_LANES), lambda j, i: (0, blk0 + ada_blk(j, i))),
                  pl.BlockSpec((1, ADA_LANES), lambda j, i: (0, blk0 + ada_blk(j, i)))],
        out_specs=[pl.BlockSpec((tm, tn), lambda j, i: (i, j)),
                   pl.BlockSpec((1, ADA_LANES), lambda j, i: (0, ada_blk(j, i)))],
        out_shape=[jax.ShapeDtypeStruct((s, n), BF16),
                   jax.ShapeDtypeStruct((1, n_ada), F32)],
        scratch_shapes=[pltpu.VMEM((k, tn), BF16)],
        compiler_params=_params("arbitrary", "arbitrary"),
        name="in_proj",
    )(h, w, cos, sin, cact, w_ada, b_ada)


def _gateproj_kernel(a_ref, w_ref, o_ref, wb_ref, *, tm, nsplit):
    _cast_weight_tile(w_ref, wb_ref)
    rm = tm // nsplit
    for r in range(nsplit):
        rows = slice(r * rm, (r + 1) * rm)
        o_ref[rows, :] = jnp.dot(a_ref[rows, :], wb_ref[...], preferred_element_type=F32)


def _gateproj(h, w):
    s, k = h.shape
    tm, tn = 1024, 512
    col0 = (3 * DIFF_WIDTH + 3 * RET_WIDTH) // tn
    return pl.pallas_call(
        functools.partial(_gateproj_kernel, tm=tm, nsplit=2),
        grid=(RET_WIDTH // tn, s // tm),
        in_specs=[pl.BlockSpec((tm, k), lambda j, i: (i, 0)),
                  pl.BlockSpec((k, tn), lambda j, i: (0, col0 + j))],
        out_specs=pl.BlockSpec((tm, tn), lambda j, i: (i, j)),
        out_shape=jax.ShapeDtypeStruct((s, RET_WIDTH), F32),
        scratch_shapes=[pltpu.VMEM((k, tn), BF16)],
        compiler_params=_params("arbitrary", "arbitrary"),
        name="gate_proj",
    )(h, w)


def _attn_kernel(q_ref, k_ref, v_ref, g_ref, lq1_ref, lk1_ref, lq2_ref, lk2_ref, o_ref, vt_ref,
                 *, tq, tk):
    i = pl.program_id(1)
    d = DIFF_HEAD_DIM
    nt = (((1,), (1,)), ((), ()))
    q = (q_ref[:, 0:d], q_ref[:, d:2 * d])

    @pl.when(i == 0)
    def _():
        vt_ref[...] = v_ref[...].T

    def scores(j):
        start = pl.multiple_of(j * tk, tk)
        kb = k_ref[pl.ds(start, tk), :]
        sp = tuple(lax.dot_general(kb[:, m * d:(m + 1) * d], q[m], nt, preferred_element_type=F32)
                   for m in range(2))
        return sp, tuple(jnp.max(x, axis=0, keepdims=True) for x in sp)

    def consume(j, s_pair, carry, masked):
        start = pl.multiple_of(j * tk, tk)
        vtb = vt_ref[:, pl.ds(start, tk)]
        new = []
        for m in range(2):
            m_old, l_old, acc_old = carry[m]
            s = s_pair[0][m]
            if masked:
                key = lax.broadcasted_iota(jnp.int32, s.shape, 0)
                qry = lax.broadcasted_iota(jnp.int32, s.shape, 1)
                s = jnp.where(key - qry <= i * tq - j * tk, s, MASK_VALUE)
                m_new = jnp.maximum(m_old, jnp.max(s, axis=0, keepdims=True))
            else:
                m_new = jnp.maximum(m_old, s_pair[1][m])
            alpha = jnp.exp2(m_old - m_new)
            p = jnp.exp2(s - m_new)
            l_new = alpha * l_old + jnp.sum(p, axis=0, keepdims=True)
            acc_new = alpha * acc_old + jnp.dot(vtb, p.astype(BF16), preferred_element_type=F32)
            new.append((m_new, l_new, acc_new))
        return tuple(new)

    init_one = (jnp.full((1, tq), MASK_VALUE, F32), jnp.zeros((1, tq), F32),
                jnp.zeros((DIFF_V_DIM, tq), F32))
    n_full = (i * tq) // tk
    n_diag = max(1, tq // tk)

    def body(j, carry):
        return consume(j, scores(j), carry, False)

    carry = lax.fori_loop(0, n_full, body, (init_one, init_one))
    for dd in range(n_diag):
        carry = consume(n_full + dd, scores(n_full + dd), carry, True)
    (_, l0, acc0), (_, l1, acc1) = carry

    lam = (jnp.exp(jnp.sum(lq1_ref[...] * lk1_ref[...], axis=-1, keepdims=True))
           - jnp.exp(jnp.sum(lq2_ref[...] * lk2_ref[...], axis=-1, keepdims=True)) + LAM_INIT)
    out = acc0 / l0 - lam * (acc1 / l1)
    y = out * lax.rsqrt(jnp.mean(out * out, axis=0, keepdims=True) + LN_EPS)
    o_ref[...] = (y * g_ref[...] * (1.0 - LAM_INIT)).T.astype(BF16)


def _diff_attention(qkv, g_col, lq1, lk1, lq2, lk2):
    s = qkv.shape[0]
    tq, tk = 1024, 1024
    k_col0 = DIFF_WIDTH // HEAD_COLS
    v_col0 = 2 * DIFF_WIDTH // HEAD_COLS
    vec = pl.BlockSpec((1, DIFF_HEAD_DIM), lambda h, i: (0, 0))
    return pl.pallas_call(
        functools.partial(_attn_kernel, tq=tq, tk=tk),
        grid=(DIFF_HEADS, s // tq),
        in_specs=[pl.BlockSpec((tq, HEAD_COLS), lambda h, i: (i, h)),
                  pl.BlockSpec((s, HEAD_COLS), lambda h, i: (0, k_col0 + h)),
                  pl.BlockSpec((s, HEAD_COLS), lambda h, i: (0, v_col0 + h)),
                  pl.BlockSpec((DIFF_V_DIM, 1), lambda h, i: (0, 0)),
                  vec, vec, vec, vec],
        out_specs=pl.BlockSpec((tq, DIFF_V_DIM), lambda h, i: (i, h)),
        out_shape=jax.ShapeDtypeStruct((s, DIFF_WIDTH), BF16),
        scratch_shapes=[pltpu.VMEM((DIFF_V_DIM, s), BF16)],
        compiler_params=_params("arbitrary", "arbitrary"),
        name="diff_attn",
    )(qkv, qkv, qkv, g_col, lq1, lk1, lq2, lk2)


def _retention_kernel(q_ref, k_ref, v_ref, rg_ref, decay_ref, xi_ref, zeta_ref, gam_ref,
                      g_ref, b_ref, o_ref, state_ref):
    @pl.when(pl.program_id(0) == 0)
    def _():
        state_ref[...] = jnp.zeros_like(state_ref)

    dh = RET_HEAD_DIM
    for h in range(RET_HEADS):
        cols = slice(h * dh, (h + 1) * dh)
        qc = q_ref[:, cols]
        kc = k_ref[:, cols]
        vc = v_ref[:, cols]
        st = state_ref[h]
        inner = lax.dot_general(qc, kc, (((1,), (1,)), ((), ())),
                                preferred_element_type=F32) * decay_ref[h]
        o = (jnp.dot(inner.astype(BF16), vc, preferred_element_type=F32)
             + jnp.dot(qc, st.astype(BF16), preferred_element_type=F32) * xi_ref[h])
        kz = (kc.astype(F32) * zeta_ref[h]).astype(BF16)
        state_ref[h] = st * gam_ref[h] + lax.dot_general(
            kz, vc, (((0,), (0,)), ((), ())), preferred_element_type=F32)
        mu = jnp.mean(o, axis=-1, keepdims=True)
        cen = o - mu
        var = jnp.mean(cen * cen, axis=-1, keepdims=True)
        y = cen * lax.rsqrt(var + LN_EPS) * g_ref[:, cols] + b_ref[:, cols]
        o_ref[:, cols] = (_silu(rg_ref[:, cols]) * y).astype(BF16)


def _retention(qkv, rg, g, b):
    s = qkv.shape[0]
    c = RET_CHUNK
    hh = RET_HEADS
    log_gamma = jnp.log1p(-jnp.exp2(-5.0 - jnp.arange(hh, dtype=F32)))
    idx = jnp.arange(c, dtype=F32)
    rel = idx[:, None] - idx[None, :]
    decay = jnp.where(rel >= 0, jnp.exp(log_gamma[:, None, None] * jnp.maximum(rel, 0.0)), 0.0)
    xi = jnp.exp(log_gamma[:, None] * (idx + 1.0))[:, :, None]
    zeta = jnp.exp(log_gamma[:, None] * (c - 1.0 - idx))[:, :, None]
    gam = jnp.broadcast_to(jnp.exp(log_gamma * c)[:, None, None], (hh, 1, RET_HEAD_DIM))
    q_blk = 3 * DIFF_WIDTH // RET_WIDTH
    full3 = lambda shape: pl.BlockSpec(shape, lambda n: (0, 0, 0))
    return pl.pallas_call(
        _retention_kernel,
        grid=(s // c,),
        in_specs=[pl.BlockSpec((c, RET_WIDTH), lambda n: (n, q_blk)),
                  pl.BlockSpec((c, RET_WIDTH), lambda n: (n, q_blk + 1)),
                  pl.BlockSpec((c, RET_WIDTH), lambda n: (n, q_blk + 2)),
                  pl.BlockSpec((c, RET_WIDTH), lambda n: (n, 0)),
                  full3((hh, c, c)), full3((hh, c, 1)), full3((hh, c, 1)),
                  full3((hh, 1, RET_HEAD_DIM)),
                  pl.BlockSpec((1, RET_WIDTH), lambda n: (0, 0)),
                  pl.BlockSpec((1, RET_WIDTH), lambda n: (0, 0))],
        out_specs=pl.BlockSpec((c, RET_WIDTH), lambda n: (n, 0)),
        out_shape=jax.ShapeDtypeStruct((s, RET_WIDTH), BF16),
        scratch_shapes=[pltpu.VMEM((hh, RET_HEAD_DIM, RET_HEAD_DIM), F32)],
        compiler_params=_params("arbitrary"),
        name="retention",
    )(qkv, qkv, qkv, rg, decay, xi, zeta, gam, g, b)


def _outproj_kernel(a1_ref, a2_ref, w1_ref, w2_ref, x_ref, mod_ref, o_ref, w1b_ref, w2b_ref,
                    *, gate_row, tm, nsplit):
    _cast_weight_tile(w1_ref, w1b_ref)
    _cast_weight_tile(w2_ref, w2b_ref)
    gate = mod_ref[gate_row:gate_row + 1, :]
    rm = tm // nsplit
    for r in range(nsplit):
        rows = slice(r * rm, (r + 1) * rm)
        mix = (jnp.dot(a1_ref[rows, :], w1b_ref[...], preferred_element_type=F32)
               + jnp.dot(a2_ref[rows, :], w2b_ref[...], preferred_element_type=F32))
        o_ref[rows, :] = DEEPNORM_ALPHA * x_ref[rows, :] + gate * mix


def _outproj(a1, a2, w, x, mod):
    s, d = x.shape
    tm, tn = 1024, 512
    half = a1.shape[1]
    return pl.pallas_call(
        functools.partial(_outproj_kernel, gate_row=2, tm=tm, nsplit=2),
        grid=(d // tn, s // tm),
        in_specs=[pl.BlockSpec((tm, half), lambda j, i: (i, 0)),
                  pl.BlockSpec((tm, half), lambda j, i: (i, 0)),
                  pl.BlockSpec((half, tn), lambda j, i: (0, j)),
                  pl.BlockSpec((half, tn), lambda j, i: (1, j)),
                  pl.BlockSpec((tm, tn), lambda j, i: (i, j)),
                  pl.BlockSpec((N_MOD, tn), lambda j, i: (0, j))],
        out_specs=pl.BlockSpec((tm, tn), lambda j, i: (i, j)),
        out_shape=jax.ShapeDtypeStruct((s, d), F32),
        scratch_shapes=[pltpu.VMEM((half, tn), BF16), pltpu.VMEM((half, tn), BF16)],
        compiler_params=_params("arbitrary", "arbitrary"),
        name="out_proj",
    )(a1, a2, w, w, x, mod)


def _layer_norm(y, g, b):
    mu = jnp.mean(y, axis=-1, keepdims=True)
    cen = y - mu
    var = jnp.mean(cen * cen, axis=-1, keepdims=True)
    return cen * lax.rsqrt(var + LN_EPS) * g + b


def _ln_mod_kernel(y_ref, g_ref, b_ref, mod_ref, x_ref, h_ref, *, shift_row, scale_row):
    xn = _layer_norm(y_ref[...], g_ref[...], b_ref[...])
    x_ref[...] = xn
    shift = mod_ref[shift_row:shift_row + 1, :]
    scale = mod_ref[scale_row:scale_row + 1, :]
    h_ref[...] = (xn * (1.0 + scale) + shift).astype(BF16)


def _ln_mod(y, g, b, mod):
    s, d = y.shape
    tm = 512
    row = pl.BlockSpec((tm, d), lambda i: (i, 0))
    vec = pl.BlockSpec((1, d), lambda i: (0, 0))
    return pl.pallas_call(
        functools.partial(_ln_mod_kernel, shift_row=3, scale_row=4),
        grid=(s // tm,),
        in_specs=[row, vec, vec, pl.BlockSpec((N_MOD, d), lambda i: (0, 0))],
        out_specs=[row, row],
        out_shape=[jax.ShapeDtypeStruct((s, d), F32), jax.ShapeDtypeStruct((s, d), BF16)],
        compiler_params=_params("arbitrary"),
        name="ln1_modulate2",
    )(y, g, b, mod)


def _ln_kernel(y_ref, g_ref, b_ref, o_ref):
    o_ref[...] = _layer_norm(y_ref[...], g_ref[...], b_ref[...])


def _ln(y, g, b):
    s, d = y.shape
    tm = 512
    row = pl.BlockSpec((tm, d), lambda i: (i, 0))
    vec = pl.BlockSpec((1, d), lambda i: (0, 0))
    return pl.pallas_call(
        _ln_kernel,
        grid=(s // tm,),
        in_specs=[row, vec, vec],
        out_specs=row,
        out_shape=jax.ShapeDtypeStruct((s, d), F32),
        compiler_params=_params("arbitrary"),
        name="ln2",
    )(y, g, b)


def _upproj_kernel(a_ref, wg_ref, wv_ref, cwg_ref, cwv_ref, cbg_ref, cbv_ref, wd_ref, o_ref, wdb_ref,
                   u_ref, wb_ref, *, tm, tn):
    first = pl.program_id(1) == 0
    wdb_ref[...] = wd_ref[...].astype(BF16)

    @pl.when(first)
    def _():
        wb_ref[:, 0:tn] = wg_ref[...].astype(BF16)
        wb_ref[:, tn:2 * tn] = wv_ref[...].astype(BF16)
        u_ref[0:8, :] = jnp.zeros((8, 2 * tn), F32)

    @pl.when(jnp.logical_not(first))
    def _():
        u_ref[0:8, :] = u_ref[tm:tm + 8, :]

    u_ref[8:tm + 8, :] = jnp.dot(a_ref[...], wb_ref[...], preferred_element_type=F32)
    cw = jnp.concatenate([cwg_ref[...], cwv_ref[...]], axis=1)
    cb = jnp.concatenate([cbg_ref[...], cbv_ref[...]], axis=1)
    y = (cw[0:1, :] * u_ref[6:tm + 6, :] + cw[1:2, :] * u_ref[7:tm + 7, :]
         + cw[2:3, :] * u_ref[8:tm + 8, :] + cb)
    o_ref[...] = (_silu(y[:, 0:tn]) * y[:, tn:2 * tn]).astype(BF16)


def _upproj(h, w, conv_w, conv_b, w_down, tm=1024, tn=FFN_TILE):
    s, k = h.shape
    ffn = w.shape[1] // 2
    nj, ni = ffn // tn, s // tm
    slab = ffn // (nj * ni)
    assert slab * nj * ni == ffn and slab % 16 == 0
    d_out = w_down.shape[1]
    return pl.pallas_call(
        functools.partial(_upproj_kernel, tm=tm, tn=tn),
        grid=(nj, ni),
        in_specs=[pl.BlockSpec((tm, k), lambda j, i: (i, 0)),
                  pl.BlockSpec((k, tn), lambda j, i: (0, j)),
                  pl.BlockSpec((k, tn), lambda j, i: (0, nj + j)),
                  pl.BlockSpec((CONV_WIDTH, tn), lambda j, i: (0, j)),
                  pl.BlockSpec((CONV_WIDTH, tn), lambda j, i: (0, nj + j)),
                  pl.BlockSpec((1, tn), lambda j, i: (0, j)),
                  pl.BlockSpec((1, tn), lambda j, i: (0, nj + j)),
                  pl.BlockSpec((slab, d_out), lambda j, i: (j * ni + i, 0))],
        out_specs=[pl.BlockSpec((tm, tn), lambda j, i: (i, j)),
                   pl.BlockSpec((slab, d_out), lambda j, i: (j * ni + i, 0))],
        out_shape=[jax.ShapeDtypeStruct((s, ffn), BF16),
                   jax.ShapeDtypeStruct(w_down.shape, BF16)],
        scratch_shapes=[pltpu.VMEM((tm + 8, 2 * tn), F32), pltpu.VMEM((k, 2 * tn), BF16)],
        compiler_params=_params("arbitrary", "arbitrary"),
        name="up_proj_conv_gate",
    )(h, w, w, conv_w, conv_w, conv_b, conv_b, w_down)


def _downproj_kernel(a_ref, w_ref, x_ref, mod_ref, o_ref, *, gate_row):
    f = jnp.dot(a_ref[...], w_ref[...], preferred_element_type=F32)
    gate = mod_ref[gate_row:gate_row + 1, :]
    o_ref[...] = DEEPNORM_ALPHA * x_ref[...] + gate * f


def _downproj(a, w_bf16, x, mod):
    s, d = x.shape
    k = a.shape[1]
    tm, tn = 512, 512
    return pl.pallas_call(
        functools.partial(_downproj_kernel, gate_row=5),
        grid=(d // tn, s // tm),
        in_specs=[pl.BlockSpec((tm, k), lambda j, i: (i, 0)),
                  pl.BlockSpec((k, tn), lambda j, i: (0, j)),
                  pl.BlockSpec((tm, tn), lambda j, i: (i, j)),
                  pl.BlockSpec((N_MOD, tn), lambda j, i: (0, j))],
        out_specs=pl.BlockSpec((tm, tn), lambda j, i: (i, j)),
        out_shape=jax.ShapeDtypeStruct((s, d), F32),
        compiler_params=_params("arbitrary", "arbitrary"),
        name="down_proj",
    )(a, w_bf16, x, mod)


def _rotary_tables(seq, d):
    inv_freq = ROPE_BASE ** (-jnp.arange(0, d, 2, dtype=F32) / d)
    ang = jnp.arange(seq, dtype=F32)[:, None] * inv_freq[None, :]
    return jnp.cos(ang), jnp.sin(ang)


def kernel(x, c, w_ada, b_ada, w_in, lambda_q1, lambda_k1, lambda_q2, lambda_k2, diff_norm_g,
           ret_norm_g, ret_norm_b, w_out, ln1_g, ln1_b, w_up, conv_w, conv_b, w_down, ln2_g, ln2_b):
    batch, seq, d = x.shape
    assert (batch, seq, d) == (1, SEQ, D_MODEL) and w_ada.shape[0] == DEPTH
    l = 0
    x2 = x.reshape(seq, d)
    cos, sin = _rotary_tables(seq, RET_HEAD_DIM)

    n_head = 2 * d
    b_ada_row = b_ada[l][None, :]
    mod_head, cact = _ada_head(c.reshape(d, 1), w_ada[l], b_ada_row, n_head)

    h1 = _modulate(x2, mod_head.reshape(2, d), shift_row=0, scale_row=1)
    qkv, mod_rest = _inproj(h1, w_in[l], cos, sin, cact, w_ada[l], b_ada_row, n_head)
    mod = jnp.concatenate([mod_head, mod_rest], axis=1).reshape(N_MOD, d)
    rg = _gateproj(h1, w_in[l])
    a_out = _diff_attention(qkv, diff_norm_g[l][:, None], lambda_q1[l][None, :], lambda_k1[l][None, :],
                            lambda_q2[l][None, :], lambda_k2[l][None, :])
    r_out = _retention(qkv, rg, ret_norm_g[l][None, :], ret_norm_b[l][None, :])
    y1 = _outproj(a_out, r_out, w_out[l], x2, mod)
    x1, h2 = _ln_mod(y1, ln1_g[l][None, :], ln1_b[l][None, :], mod)

    gated, w_down_b = _upproj(h2, w_up[l], conv_w[l], conv_b[l][None, :], w_down[l])
    y2 = _downproj(gated, w_down_b, x1, mod)
    out = _ln(y2, ln2_g[l][None, :], ln2_b[l][None, :])
    return out.reshape(batch, seq, d)
```

```python
import functools
import math

import jax
import jax.numpy as jnp
from jax import lax
from jax.experimental import pallas as pl
from jax.experimental.pallas import tpu as pltpu

F32 = jnp.float32
BF16 = jnp.bfloat16

D_MODEL = 4096
SEQ = 8192
DIFF_WIDTH = D_MODEL // 2
RET_WIDTH = D_MODEL - DIFF_WIDTH
DIFF_HEADS = 8
DIFF_HEAD_DIM = DIFF_WIDTH // (2 * DIFF_HEADS)
DIFF_V_DIM = 2 * DIFF_HEAD_DIM
RET_HEADS = 8
RET_HEAD_DIM = RET_WIDTH // RET_HEADS
FFN_DIM = ((8 * D_MODEL // 3 + 255) // 256) * 256
CONV_WIDTH = 3
RET_CHUNK = 128
ROPE_BASE = 10000.0
LN_EPS = 1e-5
N_MOD = 6
DEPTH = 1
DEEPNORM_ALPHA = (2.0 * DEPTH) ** 0.25
LAM_INIT = 0.8 - 0.6 * math.exp(-0.3 * 0)

V7X_VMEM_BYTES = 64 * 1024 * 1024
VMEM_LIMIT = V7X_VMEM_BYTES - 8 * 1024 * 1024

HEAD_COLS = 256
FFN_TILE = 256
MASK_VALUE = -1e30


def _params(*sem):
    return pltpu.CompilerParams(dimension_semantics=sem, vmem_limit_bytes=VMEM_LIMIT)


def _silu(v):
    return v / (1.0 + jnp.exp(-v))


ADA_LANES = 128


def _ada_kernel(c_ref, w_ref, b_ref, o_ref, cact_ref):
    cact = _silu(c_ref[...])
    o_ref[...] = jnp.sum(w_ref[...] * cact, axis=0, keepdims=True) + b_ref[...]
    cact_ref[...] = jnp.broadcast_to(cact, cact_ref.shape)


def _ada_head(c_col, w_ada, b_ada, n_cols):
    d = w_ada.shape[0]
    tn = 512
    return pl.pallas_call(
        _ada_kernel,
        grid=(n_cols // tn,),
        in_specs=[pl.BlockSpec((d, 1), lambda j: (0, 0)),
                  pl.BlockSpec((d, tn), lambda j: (0, j)),
                  pl.BlockSpec((1, tn), lambda j: (0, j))],
        out_specs=[pl.BlockSpec((1, tn), lambda j: (0, j)),
                   pl.BlockSpec((d, ADA_LANES), lambda j: (0, 0))],
        out_shape=[jax.ShapeDtypeStruct((1, n_cols), F32),
                   jax.ShapeDtypeStruct((d, ADA_LANES), F32)],
        compiler_params=_params("arbitrary"),
        name="ada_mod",
    )(c_col, w_ada, b_ada)


def _modulate_kernel(x_ref, mod_ref, o_ref, *, shift_row, scale_row):
    shift = mod_ref[shift_row:shift_row + 1, :]
    scale = mod_ref[scale_row:scale_row + 1, :]
    o_ref[...] = (x_ref[...] * (1.0 + scale) + shift).astype(BF16)


def _modulate(x, mod, shift_row, scale_row):
    s, d = x.shape
    tm = 512
    return pl.pallas_call(
        functools.partial(_modulate_kernel, shift_row=shift_row, scale_row=scale_row),
        grid=(s // tm,),
        in_specs=[pl.BlockSpec((tm, d), lambda i: (i, 0)),
                  pl.BlockSpec(mod.shape, lambda i: (0, 0))],
        out_specs=pl.BlockSpec((tm, d), lambda i: (i, 0)),
        out_shape=jax.ShapeDtypeStruct((s, d), BF16),
        compiler_params=_params("arbitrary"),
        name="modulate1",
    )(x, mod)


def _cast_weight_tile(w_ref, wb_ref):
    @pl.when(pl.program_id(1) == 0)
    def _():
        wb_ref[...] = w_ref[...].astype(BF16)


def _inproj_kernel(a_ref, w_ref, cos_ref, sin_ref, cact_ref, wada_ref, bada_ref, o_ref, mod_ref,
                   wb_ref, *, tn, tm, nsplit):
    j = pl.program_id(0)
    per_group = DIFF_WIDTH // tn
    _cast_weight_tile(w_ref, wb_ref)

    def ada_slab():
        groups = 16
        rws = wada_ref.shape[0] // groups
        parts = [jnp.sum(wada_ref[g * rws:(g + 1) * rws, :] * cact_ref[g * rws:(g + 1) * rws, :],
                         axis=0, keepdims=True) for g in range(groups)]
        while len(parts) > 1:
            parts = [parts[p] + parts[p + 1] for p in range(0, len(parts), 2)]
        mod_ref[...] = parts[0] + bada_ref[...]

    is_rot = jnp.logical_and(j >= 3 * per_group, j < 5 * per_group)
    qscale = jnp.where(j < per_group, DIFF_HEAD_DIM ** -0.5 * math.log2(math.e), 1.0).astype(F32)
    kscale = jnp.where(j >= 4 * per_group, RET_HEAD_DIM ** -0.5, 1.0).astype(F32)
    half = RET_HEAD_DIM // 2
    rm = tm // nsplit

    def rows(r):
        return slice(r * rm, (r + 1) * rm)

    @pl.when(jnp.logical_not(is_rot))
    def _():
        ada_slab()
        for r in range(nsplit):
            acc = jnp.dot(a_ref[rows(r), :], wb_ref[...], preferred_element_type=F32)
            o_ref[rows(r), :] = (acc * qscale).astype(BF16)

    @pl.when(is_rot)
    def _():
        ada_slab()
        for r in range(nsplit):
            acc = jnp.dot(a_ref[rows(r), :], wb_ref[...], preferred_element_type=F32)
            cos = cos_ref[rows(r), :] * kscale
            sin = sin_ref[rows(r), :] * kscale
            for h in range(tn // RET_HEAD_DIM):
                lo = h * RET_HEAD_DIM
                x1 = acc[:, lo:lo + half]
                x2 = acc[:, lo + half:lo + 2 * half]
                o_ref[rows(r), lo:lo + half] = (x1 * cos - x2 * sin).astype(BF16)
                o_ref[rows(r), lo + half:lo + 2 * half] = (x2 * cos + x1 * sin).astype(BF16)


def _inproj(h, w, cos, sin, cact, w_ada, b_ada, ada_col0):
    s, k = h.shape
    n = 3 * DIFF_WIDTH + 3 * RET_WIDTH
    tm, tn = 1024, 512
    ni = s // tm
    n_ada = w_ada.shape[1] - ada_col0
    ada_steps = n_ada // ADA_LANES
    assert ada_steps * ADA_LANES == n_ada and ada_steps <= (n // tn) * ni
    blk0 = ada_col0 // ADA_LANES

    def ada_blk(j, i):
        return jnp.minimum(j * ni + i, ada_steps - 1)

    return pl.pallas_call(
        functools.partial(_inproj_kernel, tn=tn, tm=tm, nsplit=2),
        grid=(n // tn, ni),
        in_specs=[pl.BlockSpec((tm, k), lambda j, i: (i, 0)),
                  pl.BlockSpec((k, tn), lambda j, i: (0, j)),
                  pl.BlockSpec((tm, RET_HEAD_DIM // 2), lambda j, i: (i, 0)),
                  pl.BlockSpec((tm, RET_HEAD_DIM // 2), lambda j, i: (i, 0)),
                  pl.BlockSpec((k, ADA_LANES), lambda j, i: (0, 0)),
                  pl.BlockSpec((k, ADA_LANES), lambda j, i: (0, blk0 + ada_blk(j, i))),
                  pl.BlockSpec((1, ADA_LANES), lambda j, i: (0, blk0 + ada_blk(j, i)))],
        out_specs=[pl.BlockSpec((tm, tn), lambda j, i: (i, j)),
                   pl.BlockSpec((1, ADA_LANES), lambda j, i: (0, ada_blk(j, i)))],
        out_shape=[jax.ShapeDtypeStruct((s, n), BF16),
                   jax.ShapeDtypeStruct((1, n_ada), F32)],
        scratch_shapes=[pltpu.VMEM((k, tn), BF16)],
        compiler_params=_params("arbitrary", "arbitrary"),
        name="in_proj",
    )(h, w, cos, sin, cact, w_ada, b_ada)


def _gateproj_kernel(a_ref, w_ref, o_ref, wb_ref, *, tm, nsplit):
    _cast_weight_tile(w_ref, wb_ref)
    rm = tm // nsplit
    for r in range(nsplit):
        rows = slice(r * rm, (r + 1) * rm)
        o_ref[rows, :] = jnp.dot(a_ref[rows, :], wb_ref[...], preferred_element_type=F32)


def _gateproj(h, w):
    s, k = h.shape
    tm, tn = 1024, 512
    col0 = (3 * DIFF_WIDTH + 3 * RET_WIDTH) // tn
    return pl.pallas_call(
        functools.partial(_gateproj_kernel, tm=tm, nsplit=2),
        grid=(RET_WIDTH // tn, s // tm),
        in_specs=[pl.BlockSpec((tm, k), lambda j, i: (i, 0)),
                  pl.BlockSpec((k, tn), lambda j, i: (0, col0 + j))],
        out_specs=pl.BlockSpec((tm, tn), lambda j, i: (i, j)),
        out_shape=jax.ShapeDtypeStruct((s, RET_WIDTH), F32),
        scratch_shapes=[pltpu.VMEM((k, tn), BF16)],
        compiler_params=_params("arbitrary", "arbitrary"),
        name="gate_proj",
    )(h, w)


def _attn_kernel(q_ref, k_ref, v_ref, g_ref, lq1_ref, lk1_ref, lq2_ref, lk2_ref, o_ref, vt_ref,
                 *, tq, tk):
    i = pl.program_id(1)
    d = DIFF_HEAD_DIM
    nt = (((1,), (1,)), ((), ()))
    q = (q_ref[:, 0:d], q_ref[:, d:2 * d])

    @pl.when(i == 0)
    def _():
        vt_ref[...] = v_ref[...].T

    def scores(j):
        start = pl.multiple_of(j * tk, tk)
        kb = k_ref[pl.ds(start, tk), :]
        sp = tuple(lax.dot_general(kb[:, m * d:(m + 1) * d], q[m], nt, preferred_element_type=F32)
                   for m in range(2))
        return sp, tuple(jnp.max(x, axis=0, keepdims=True) for x in sp)

    def consume(j, s_pair, carry, masked):
        start = pl.multiple_of(j * tk, tk)
        vtb = vt_ref[:, pl.ds(start, tk)]
        new = []
        for m in range(2):
            m_old, l_old, acc_old = carry[m]
            s = s_pair[0][m]
            if masked:
                key = lax.broadcasted_iota(jnp.int32, s.shape, 0)
                qry = lax.broadcasted_iota(jnp.int32, s.shape, 1)
                s = jnp.where(key - qry <= i * tq - j * tk, s, MASK_VALUE)
                m_new = jnp.maximum(m_old, jnp.max(s, axis=0, keepdims=True))
            else:
                m_new = jnp.maximum(m_old, s_pair[1][m])
            alpha = jnp.exp2(m_old - m_new)
            p = jnp.exp2(s - m_new)
            l_new = alpha * l_old + jnp.sum(p, axis=0, keepdims=True)
            acc_new = alpha * acc_old + jnp.dot(vtb, p.astype(BF16), preferred_element_type=F32)
            new.append((m_new, l_new, acc_new))
        return tuple(new)

    init_one = (jnp.full((1, tq), MASK_VALUE, F32), jnp.zeros((1, tq), F32),
                jnp.zeros((DIFF_V_DIM, tq), F32))
    n_full = (i * tq) // tk
    n_diag = max(1, tq // tk)

    def body(j, carry):
        return consume(j, scores(j), carry, False)

    carry = lax.fori_loop(0, n_full, body, (init_one, init_one))
    for dd in range(n_diag):
        carry = consume(n_full + dd, scores(n_full + dd), carry, True)
    (_, l0, acc0), (_, l1, acc1) = carry

    lam = (jnp.exp(jnp.sum(lq1_ref[...] * lk1_ref[...], axis=-1, keepdims=True))
           - jnp.exp(jnp.sum(lq2_ref[...] * lk2_ref[...], axis=-1, keepdims=True)) + LAM_INIT)
    out = acc0 / l0 - lam * (acc1 / l1)
    y = out * lax.rsqrt(jnp.mean(out * out, axis=0, keepdims=True) + LN_EPS)
    o_ref[...] = (y * g_ref[...] * (1.0 - LAM_INIT)).T.astype(BF16)


def _diff_attention(qkv, g_col, lq1, lk1, lq2, lk2):
    s = qkv.shape[0]
    tq, tk = 1024, 1024
    k_col0 = DIFF_WIDTH // HEAD_COLS
    v_col0 = 2 * DIFF_WIDTH // HEAD_COLS
    vec = pl.BlockSpec((1, DIFF_HEAD_DIM), lambda h, i: (0, 0))
    return pl.pallas_call(
        functools.partial(_attn_kernel, tq=tq, tk=tk),
        grid=(DIFF_HEADS, s // tq),
        in_specs=[pl.BlockSpec((tq, HEAD_COLS), lambda h, i: (i, h)),
                  pl.BlockSpec((s, HEAD_COLS), lambda h, i: (0, k_col0 + h)),
                  pl.BlockSpec((s, HEAD_COLS), lambda h, i: (0, v_col0 + h)),
                  pl.BlockSpec((DIFF_V_DIM, 1), lambda h, i: (0, 0)),
                  vec, vec, vec, vec],
        out_specs=pl.BlockSpec((tq, DIFF_V_DIM), lambda h, i: (i, h)),
        out_shape=jax.ShapeDtypeStruct((s, DIFF_WIDTH), BF16),
        scratch_shapes=[pltpu.VMEM((DIFF_V_DIM, s), BF16)],
        compiler_params=_params("arbitrary", "arbitrary"),
        name="diff_attn",
    )(qkv, qkv, qkv, g_col, lq1, lk1, lq2, lk2)


def _retention_kernel(q_ref, k_ref, v_ref, rg_ref, decay_ref, xi_ref, zeta_ref, gam_ref,
                      g_ref, b_ref, o_ref, state_ref):
    @pl.when(pl.program_id(0) == 0)
    def _():
        state_ref[...] = jnp.zeros_like(state_ref)

    dh = RET_HEAD_DIM
    for h in range(RET_HEADS):
        cols = slice(h * dh, (h + 1) * dh)
        qc = q_ref[:, cols]
        kc = k_ref[:, cols]
        vc = v_ref[:, cols]
        st = state_ref[h]
        inner = lax.dot_general(qc, kc, (((1,), (1,)), ((), ())),
                                preferred_element_type=F32) * decay_ref[h]
        o = (jnp.dot(inner.astype(BF16), vc, preferred_element_type=F32)
             + jnp.dot(qc, st.astype(BF16), preferred_element_type=F32) * xi_ref[h])
        kz = (kc.astype(F32) * zeta_ref[h]).astype(BF16)
        state_ref[h] = st * gam_ref[h] + lax.dot_general(
            kz, vc, (((0,), (0,)), ((), ())), preferred_element_type=F32)
        mu = jnp.mean(o, axis=-1, keepdims=True)
        cen = o - mu
        var = jnp.mean(cen * cen, axis=-1, keepdims=True)
        y = cen * lax.rsqrt(var + LN_EPS) * g_ref[:, cols] + b_ref[:, cols]
        o_ref[:, cols] = (_silu(rg_ref[:, cols]) * y).astype(BF16)


def _retention(qkv, rg, g, b):
    s = qkv.shape[0]
    c = RET_CHUNK
    hh = RET_HEADS
    log_gamma = jnp.log1p(-jnp.exp2(-5.0 - jnp.arange(hh, dtype=F32)))
    idx = jnp.arange(c, dtype=F32)
    rel = idx[:, None] - idx[None, :]
    decay = jnp.where(rel >= 0, jnp.exp(log_gamma[:, None, None] * jnp.maximum(rel, 0.0)), 0.0)
    xi = jnp.exp(log_gamma[:, None] * (idx + 1.0))[:, :, None]
    zeta = jnp.exp(log_gamma[:, None] * (c - 1.0 - idx))[:, :, None]
    gam = jnp.broadcast_to(jnp.exp(log_gamma * c)[:, None, None], (hh, 1, RET_HEAD_DIM))
    q_blk = 3 * DIFF_WIDTH // RET_WIDTH
    full3 = lambda shape: pl.BlockSpec(shape, lambda n: (0, 0, 0))
    return pl.pallas_call(
        _retention_kernel,
        grid=(s // c,),
        in_specs=[pl.BlockSpec((c, RET_WIDTH), lambda n: (n, q_blk)),
                  pl.BlockSpec((c, RET_WIDTH), lambda n: (n, q_blk + 1)),
                  pl.BlockSpec((c, RET_WIDTH), lambda n: (n, q_blk + 2)),
                  pl.BlockSpec((c, RET_WIDTH), lambda n: (n, 0)),
                  full3((hh, c, c)), full3((hh, c, 1)), full3((hh, c, 1)),
                  full3((hh, 1, RET_HEAD_DIM)),
                  pl.BlockSpec((1, RET_WIDTH), lambda n: (0, 0)),
                  pl.BlockSpec((1, RET_WIDTH), lambda n: (0, 0))],
        out_specs=pl.BlockSpec((c, RET_WIDTH), lambda n: (n, 0)),
        out_shape=jax.ShapeDtypeStruct((s, RET_WIDTH), BF16),
        scratch_shapes=[pltpu.VMEM((hh, RET_HEAD_DIM, RET_HEAD_DIM), F32)],
        compiler_params=_params("arbitrary"),
        name="retention",
    )(qkv, qkv, qkv, rg, decay, xi, zeta, gam, g, b)


def _outproj_kernel(a1_ref, a2_ref, w1_ref, w2_ref, x_ref, mod_ref, o_ref, w1b_ref, w2b_ref,
                    *, gate_row, tm, nsplit):
    _cast_weight_tile(w1_ref, w1b_ref)
    _cast_weight_tile(w2_ref, w2b_ref)
    gate = mod_ref[gate_row:gate_row + 1, :]
    rm = tm // nsplit
    for r in range(nsplit):
        rows = slice(r * rm, (r + 1) * rm)
        mix = (jnp.dot(a1_ref[rows, :], w1b_ref[...], preferred_element_type=F32)
               + jnp.dot(a2_ref[rows, :], w2b_ref[...], preferred_element_type=F32))
        o_ref[rows, :] = DEEPNORM_ALPHA * x_ref[rows, :] + gate * mix


def _outproj(a1, a2, w, x, mod):
    s, d = x.shape
    tm, tn = 1024, 512
    half = a1.shape[1]
    return pl.pallas_call(
        functools.partial(_outproj_kernel, gate_row=2, tm=tm, nsplit=2),
        grid=(d // tn, s // tm),
        in_specs=[pl.BlockSpec((tm, half), lambda j, i: (i, 0)),
                  pl.BlockSpec((tm, half), lambda j, i: (i, 0)),
                  pl.BlockSpec((half, tn), lambda j, i: (0, j)),
                  pl.BlockSpec((half, tn), lambda j, i: (1, j)),
                  pl.BlockSpec((tm, tn), lambda j, i: (i, j)),
                  pl.BlockSpec((N_MOD, tn), lambda j, i: (0, j))],
        out_specs=pl.BlockSpec((tm, tn), lambda j, i: (i, j)),
        out_shape=jax.ShapeDtypeStruct((s, d), F32),
        scratch_shapes=[pltpu.VMEM((half, tn), BF16), pltpu.VMEM((half, tn), BF16)],
        compiler_params=_params("arbitrary", "arbitrary"),
        name="out_proj",
    )(a1, a2, w, w, x, mod)


def _layer_norm(y, g, b):
    mu = jnp.mean(y, axis=-1, keepdims=True)
    cen = y - mu
    var = jnp.mean(cen * cen, axis=-1, keepdims=True)
    return cen * lax.rsqrt(var + LN_EPS) * g + b


def _ln_mod_kernel(y_ref, g_ref, b_ref, mod_ref, x_ref, h_ref, *, shift_row, scale_row):
    xn = _layer_norm(y_ref[...], g_ref[...], b_ref[...])
    x_ref[...] = xn
    shift = mod_ref[shift_row:shift_row + 1, :]
    scale = mod_ref[scale_row:scale_row + 1, :]
    h_ref[...] = (xn * (1.0 + scale) + shift).astype(BF16)


def _ln_mod(y, g, b, mod):
    s, d = y.shape
    tm = 512
    row = pl.BlockSpec((tm, d), lambda i: (i, 0))
    vec = pl.BlockSpec((1, d), lambda i: (0, 0))
    return pl.pallas_call(
        functools.partial(_ln_mod_kernel, shift_row=3, scale_row=4),
        grid=(s // tm,),
        in_specs=[row, vec, vec, pl.BlockSpec((N_MOD, d), lambda i: (0, 0))],
        out_specs=[row, row],
        out_shape=[jax.ShapeDtypeStruct((s, d), F32), jax.ShapeDtypeStruct((s, d), BF16)],
        compiler_params=_params("arbitrary"),
        name="ln1_modulate2",
    )(y, g, b, mod)


def _ln_kernel(y_ref, g_ref, b_ref, o_ref):
    o_ref[...] = _layer_norm(y_ref[...], g_ref[...], b_ref[...])


def _ln(y, g, b):
    s, d = y.shape
    tm = 512
    row = pl.BlockSpec((tm, d), lambda i: (i, 0))
    vec = pl.BlockSpec((1, d), lambda i: (0, 0))
    return pl.pallas_call(
        _ln_kernel,
        grid=(s // tm,),
        in_specs=[row, vec, vec],
        out_specs=row,
        out_shape=jax.ShapeDtypeStruct((s, d), F32),
        compiler_params=_params("arbitrary"),
        name="ln2",
    )(y, g, b)


def _upproj_kernel(a_ref, wg_ref, wv_ref, cwg_ref, cwv_ref, cbg_ref, cbv_ref, wd_ref, o_ref, wdb_ref,
                   u_ref, wb_ref, *, tm, tn):
    first = pl.program_id(1) == 0
    wdb_ref[...] = wd_ref[...].astype(BF16)

    @pl.when(first)
    def _():
        wb_ref[:, 0:tn] = wg_ref[...].astype(BF16)
        wb_ref[:, tn:2 * tn] = wv_ref[...].astype(BF16)
        u_ref[0:8, :] = jnp.zeros((8, 2 * tn), F32)

    @pl.when(jnp.logical_not(first))
    def _():
        u_ref[0:8, :] = u_ref[tm:tm + 8, :]

    u_ref[8:tm + 8, :] = jnp.dot(a_ref[...], wb_ref[...], preferred_element_type=F32)
    cw = jnp.concatenate([cwg_ref[...], cwv_ref[...]], axis=1)
    cb = jnp.concatenate([cbg_ref[...], cbv_ref[...]], axis=1)
    y = (cw[0:1, :] * u_ref[6:tm + 6, :] + cw[1:2, :] * u_ref[7:tm + 7, :]
         + cw[2:3, :] * u_ref[8:tm + 8, :] + cb)
    o_ref[...] = (_silu(y[:, 0:tn]) * y[:, tn:2 * tn]).astype(BF16)


def _upproj(h, w, conv_w, conv_b, w_down, tm=1024, tn=FFN_TILE):
    s, k = h.shape
    ffn = w.shape[1] // 2
    nj, ni = ffn // tn, s // tm
    slab = ffn // (nj * ni)
    assert slab * nj * ni == ffn and slab % 16 == 0
    d_out = w_down.shape[1]
    return pl.pallas_call(
        functools.partial(_upproj_kernel, tm=tm, tn=tn),
        grid=(nj, ni),
        in_specs=[pl.BlockSpec((tm, k), lambda j, i: (i, 0)),
                  pl.BlockSpec((k, tn), lambda j, i: (0, j)),
                  pl.BlockSpec((k, tn), lambda j, i: (0, nj + j)),
                  pl.BlockSpec((CONV_WIDTH, tn), lambda j, i: (0, j)),
                  pl.BlockSpec((CONV_WIDTH, tn), lambda j, i: (0, nj + j)),
                  pl.BlockSpec((1, tn), lambda j, i: (0, j)),
                  pl.BlockSpec((1, tn), lambda j, i: (0, nj + j)),
                  pl.BlockSpec((slab, d_out), lambda j, i: (j * ni + i, 0))],
        out_specs=[pl.BlockSpec((tm, tn), lambda j, i: (i, j)),
                   pl.BlockSpec((slab, d_out), lambda j, i: (j * ni + i, 0))],
        out_shape=[jax.ShapeDtypeStruct((s, ffn), BF16),
                   jax.ShapeDtypeStruct(w_down.shape, BF16)],
        scratch_shapes=[pltpu.VMEM((tm + 8, 2 * tn), F32), pltpu.VMEM((k, 2 * tn), BF16)],
        compiler_params=_params("arbitrary", "arbitrary"),
        name="up_proj_conv_gate",
    )(h, w, w, conv_w, conv_w, conv_b, conv_b, w_down)


def _downproj_kernel(a_ref, w_ref, x_ref, mod_ref, o_ref, *, gate_row):
    f = jnp.dot(a_ref[...], w_ref[...], preferred_element_type=F32)
    gate = mod_ref[gate_row:gate_row + 1, :]
    o_ref[...] = DEEPNORM_ALPHA * x_ref[...] + gate * f


def _downproj(a, w_bf16, x, mod):
    s, d = x.shape
    k = a.shape[1]
    tm, tn = 512, 512
    return pl.pallas_call(
        functools.partial(_downproj_kernel, gate_row=5),
        grid=(d // tn, s // tm),
        in_specs=[pl.BlockSpec((tm, k), lambda j, i: (i, 0)),
                  pl.BlockSpec((k, tn), lambda j, i: (0, j)),
                  pl.BlockSpec((tm, tn), lambda j, i: (i, j)),
                  pl.BlockSpec((N_MOD, tn), lambda j, i: (0, j))],
        out_specs=pl.BlockSpec((tm, tn), lambda j, i: (i, j)),
        out_shape=jax.ShapeDtypeStruct((s, d), F32),
        compiler_params=_params("arbitrary", "arbitrary"),
        name="down_proj",
    )(a, w_bf16, x, mod)


def _rotary_tables(seq, d):
    inv_freq = ROPE_BASE ** (-jnp.arange(0, d, 2, dtype=F32) / d)
    ang = jnp.arange(seq, dtype=F32)[:, None] * inv_freq[None, :]
    return jnp.cos(ang), jnp.sin(ang)


def kernel(x, c, w_ada, b_ada, w_in, lambda_q1, lambda_k1, lambda_q2, lambda_k2, diff_norm_g,
           ret_norm_g, ret_norm_b, w_out, ln1_g, ln1_b, w_up, conv_w, conv_b, w_down, ln2_g, ln2_b):
    batch, seq, d = x.shape
    assert (batch, seq, d) == (1, SEQ, D_MODEL) and w_ada.shape[0] == DEPTH
    assert w_up.shape[-1] == 2 * FFN_DIM and w_down.shape[-2] == FFN_DIM
    l = 0
    x2 = x.reshape(seq, d)
    cos, sin = _rotary_tables(seq, RET_HEAD_DIM)

    n_head = 2 * d
    b_ada_row = b_ada[l][None, :]
    mod_head, cact = _ada_head(c.reshape(d, 1), w_ada[l], b_ada_row, n_head)

    h1 = _modulate(x2, mod_head.reshape(2, d), shift_row=0, scale_row=1)
    qkv, mod_rest = _inproj(h1, w_in[l], cos, sin, cact, w_ada[l], b_ada_row, n_head)
    mod = jnp.concatenate([mod_head, mod_rest], axis=1).reshape(N_MOD, d)
    rg = _gateproj(h1, w_in[l])
    a_out = _diff_attention(qkv, diff_norm_g[l][:, None], lambda_q1[l][None, :], lambda_k1[l][None, :],
                            lambda_q2[l][None, :], lambda_k2[l][None, :])
    r_out = _retention(qkv, rg, ret_norm_g[l][None, :], ret_norm_b[l][None, :])
    y1 = _outproj(a_out, r_out, w_out[l], x2, mod)
    x1, h2 = _ln_mod(y1, ln1_g[l][None, :], ln1_b[l][None, :], mod)

    gated, w_down_b = _upproj(h2, w_up[l], conv_w[l], conv_b[l][None, :], w_down[l])
    y2 = _downproj(gated, w_down_b, x1, mod)
    out = _ln(y2, ln2_g[l][None, :], ln2_b[l][None, :])
    return out.reshape(batch, seq, d)
```
